```python
import jax, jax.numpy as jnp
from jax import lax
import numpy as np

D_MODEL = 1024
BATCH = 8
SEQ = 2048
DEPTH = 1
DEC_BATCH = 128
DEC_SEQ = 8
PAST_LEN = 16384
PAGE_SIZE = 128

D_LRU = D_MODEL
N_LRU_HEADS = 8
LRU_HEAD_DIM = D_LRU // N_LRU_HEADS
LRU_CONV_W = 4
LRU_C = 8.0
D_CONV = D_MODEL
CONF_CONV_W = 31
D_MIX = D_LRU + D_CONV
D_IN = 2 * D_LRU + 2 * D_CONV
D_FF = 2816
FFN_RES = 0.5
N_MEM = 256
N_MEM_HEADS = 4
MEM_HEAD_DIM = D_MODEL // N_MEM_HEADS
EPS = 1e-6

kernel_name = "hymba_rglru_conformer_conv_decoder_step"


def rmsnorm(x, g):
    xf = x.astype(jnp.float32)
    y = xf * lax.rsqrt(jnp.mean(xf * xf, axis=-1, keepdims=True) + EPS)
    return (y * g.astype(jnp.float32)).astype(x.dtype)


def layernorm(x, g, b):
    xf = x.astype(jnp.float32)
    xc = xf - jnp.mean(xf, axis=-1, keepdims=True)
    y = xc * lax.rsqrt(jnp.mean(xc * xc, axis=-1, keepdims=True) + EPS)
    return (y * g.astype(jnp.float32) + b.astype(jnp.float32)).astype(x.dtype)


def swiglu(x, w_gate, w_up, w_down):
    return (jax.nn.silu(x @ w_gate) * (x @ w_up)) @ w_down


def causal_dwconv(x, buf, w, b):
    xp = jnp.concatenate([buf.astype(x.dtype), x], axis=1)
    y = lax.conv_general_dilated(
        xp, w[:, None, :].astype(x.dtype), window_strides=(1,), padding='VALID',
        dimension_numbers=('NWC', 'WIO', 'NWC'), feature_group_count=x.shape[-1])
    new_buf = xp[:, xp.shape[1] - (w.shape[0] - 1):, :]
    return y + b.astype(x.dtype), new_buf


def rg_lru(x, h0, w_a, b_a, w_x, b_x, lam):
    bsz, t_len, _ = x.shape
    xh = x.reshape(bsz, t_len, N_LRU_HEADS, LRU_HEAD_DIM)
    r = jax.nn.sigmoid(jnp.einsum('bthi,hij->bthj', xh, w_a) + b_a.reshape(N_LRU_HEADS, LRU_HEAD_DIM))
    i = jax.nn.sigmoid(jnp.einsum('bthi,hij->bthj', xh, w_x) + b_x.reshape(N_LRU_HEADS, LRU_HEAD_DIM))
    r = r.reshape(bsz, t_len, D_LRU).astype(jnp.float32)
    i = i.reshape(bsz, t_len, D_LRU).astype(jnp.float32)
    log_a = -LRU_C * r * jax.nn.softplus(-lam.astype(jnp.float32))
    a = jnp.exp(log_a)
    u = jnp.sqrt(-jnp.expm1(2.0 * log_a)) * (i * x.astype(jnp.float32))

    def combine(left, right):
        a_l, b_l = left
        a_r, b_r = right
        return a_l * a_r, a_r * b_l + b_r

    a_cum, b_cum = lax.associative_scan(combine, (a, u), axis=1)
    h = a_cum * h0.astype(jnp.float32)[:, None, :] + b_cum
    return h.astype(x.dtype), h[:, -1].astype(x.dtype)


def token_mixer(h, conv4_buf, lru_h0, conv31_buf, p):
    z = h @ p['w_in']
    z_x, z_g, z_v, z_gate = jnp.split(z, [D_LRU, 2 * D_LRU, 2 * D_LRU + D_CONV], axis=-1)
    xc, new_conv4 = causal_dwconv(z_x, conv4_buf, p['lru_conv_w'], p['lru_conv_b'])
    lru, new_h = rg_lru(xc, lru_h0, p['lru_w_a'], p['lru_b_a'], p['lru_w_x'], p['lru_b_x'], p['lru_lambda'])
    y_lru = jax.nn.gelu(z_g) * lru
    glu = z_v * jax.nn.sigmoid(z_gate)
    c, new_conv31 = causal_dwconv(glu, conv31_buf, p['conf_conv_w'], p['conf_conv_b'])
    y_conf = jax.nn.silu(layernorm(c, p['conf_ln_g'], p['conf_ln_b']))
    out = jnp.concatenate([y_lru, y_conf], axis=-1) @ p['w_out']
    return out, new_conv4, new_h, new_conv31


def memory_kv(mem, g, w_k, w_v):
    bsz = mem.shape[0]
    m = rmsnorm(mem, g)
    k = (m @ w_k).reshape(bsz, N_MEM, N_MEM_HEADS, MEM_HEAD_DIM)
    v = (m @ w_v).reshape(bsz, N_MEM, N_MEM_HEADS, MEM_HEAD_DIM)
    return k, v


def memory_attention(h, k, v, w_q, w_o):
    bsz, t_len, _ = h.shape
    q = (h @ w_q).reshape(bsz, t_len, N_MEM_HEADS, MEM_HEAD_DIM).astype(jnp.float32)
    s = jnp.einsum('bthd,bmhd->bhtm', q, k.astype(jnp.float32)) * (MEM_HEAD_DIM ** -0.5)
    prob = jax.nn.softmax(s, axis=-1)
    o = jnp.einsum('bhtm,bmhd->bthd', prob, v.astype(jnp.float32))
    return o.reshape(bsz, t_len, D_MODEL).astype(h.dtype) @ w_o


def decoder_layer(x, conv4_buf, lru_h0, conv31_buf, mem_k, mem_v, p):
    x = x + FFN_RES * rmsnorm(swiglu(rmsnorm(x, p['g_ff1_pre']), p['ff1_w_gate'], p['ff1_w_up'], p['ff1_w_down']), p['g_ff1_post'])
    mix, new_conv4, new_h, new_conv31 = token_mixer(rmsnorm(x, p['g_mix_pre']), conv4_buf, lru_h0, conv31_buf, p)
    x = x + rmsnorm(mix, p['g_mix_post'])
    x = x + rmsnorm(memory_attention(rmsnorm(x, p['g_mem_pre']), mem_k, mem_v, p['w_q'], p['w_o']), p['g_mem_post'])
    x = x + FFN_RES * rmsnorm(swiglu(rmsnorm(x, p['g_ff2_pre']), p['ff2_w_gate'], p['ff2_w_up'], p['ff2_w_down']), p['g_ff2_post'])
    return x, new_conv4, new_h, new_conv31


def setup_inputs(seed: int = 0) -> dict:
    key = jax.random.key(seed)
    ks = iter(jax.random.split(key, 48))

    def nrm(shape, scale):
        return jax.random.normal(next(ks), shape, jnp.float32) * scale

    def gain():
        return 1.0 + nrm((DEPTH, D_MODEL), 0.02)

    d = D_MODEL
    inputs = {}
    inputs['x_prompt'] = nrm((BATCH, SEQ, d), 1.0)
    inputs['x_sample'] = nrm((DEC_BATCH, DEC_SEQ, d), 1.0)
    inputs['mem_prompt'] = nrm((BATCH, N_MEM, d), 1.0)
    inputs['state_lru_conv'] = nrm((DEPTH, DEC_BATCH, LRU_CONV_W - 1, D_LRU), 1.0)
    inputs['state_lru_h'] = nrm((DEPTH, DEC_BATCH, D_LRU), 0.5)
    inputs['state_conf_conv'] = nrm((DEPTH, DEC_BATCH, CONF_CONV_W - 1, D_CONV), 0.5)
    inputs['cache_mem_k'] = nrm((DEPTH, DEC_BATCH, N_MEM, N_MEM_HEADS, MEM_HEAD_DIM), 1.0)
    inputs['cache_mem_v'] = nrm((DEPTH, DEC_BATCH, N_MEM, N_MEM_HEADS, MEM_HEAD_DIM), 1.0)
    inputs['g_ff1_pre'] = gain()
    inputs['ff1_w_gate'] = nrm((DEPTH, d, D_FF), d ** -0.5)
    inputs['ff1_w_up'] = nrm((DEPTH, d, D_FF), d ** -0.5)
    inputs['ff1_w_down'] = nrm((DEPTH, D_FF, d), D_FF ** -0.5)
    inputs['g_ff1_post'] = gain()
    inputs['g_mix_pre'] = gain()
    inputs['w_in'] = nrm((DEPTH, d, D_IN), d ** -0.5)
    inputs['lru_conv_w'] = nrm((DEPTH, LRU_CONV_W, D_LRU), LRU_CONV_W ** -0.5)
    inputs['lru_conv_b'] = nrm((DEPTH, D_LRU), 0.01)
    inputs['lru_w_a'] = nrm((DEPTH, N_LRU_HEADS, LRU_HEAD_DIM, LRU_HEAD_DIM), LRU_HEAD_DIM ** -0.5)
    inputs['lru_b_a'] = nrm((DEPTH, D_LRU), 0.01)
    inputs['lru_w_x'] = nrm((DEPTH, N_LRU_HEADS, LRU_HEAD_DIM, LRU_HEAD_DIM), LRU_HEAD_DIM ** -0.5)
    inputs['lru_b_x'] = nrm((DEPTH, D_LRU), 0.01)
    a_c = jax.random.uniform(next(ks), (DEPTH, D_LRU), jnp.float32, 0.9, 0.999)
    a0 = a_c ** (1.0 / LRU_C)
    inputs['lru_lambda'] = jnp.log(a0) - jnp.log1p(-a0)
    inputs['conf_conv_w'] = nrm((DEPTH, CONF_CONV_W, D_CONV), CONF_CONV_W ** -0.5)
    inputs['conf_conv_b'] = nrm((DEPTH, D_CONV), 0.01)
    inputs['conf_ln_g'] = 1.0 + nrm((DEPTH, D_CONV), 0.02)
    inputs['conf_ln_b'] = nrm((DEPTH, D_CONV), 0.01)
    inputs['w_out'] = nrm((DEPTH, D_MIX, d), D_MIX ** -0.5)
    inputs['g_mix_post'] = gain()
    inputs['g_mem_pre'] = gain()
    inputs['g_mem_kv'] = gain()
    inputs['w_mem_k'] = nrm((DEPTH, d, d), d ** -0.5)
    inputs['w_mem_v'] = nrm((DEPTH, d, d), d ** -0.5)
    inputs['w_q'] = nrm((DEPTH, d, d), d ** -0.5)
    inputs['w_o'] = nrm((DEPTH, d, d), d ** -0.5)
    inputs['g_mem_post'] = gain()
    inputs['g_ff2_pre'] = gain()
    inputs['ff2_w_gate'] = nrm((DEPTH, d, D_FF), d ** -0.5)
    inputs['ff2_w_up'] = nrm((DEPTH, d, D_FF), d ** -0.5)
    inputs['ff2_w_down'] = nrm((DEPTH, D_FF, d), D_FF ** -0.5)
    inputs['g_ff2_post'] = gain()
    return inputs


def reference(x_prompt, x_sample, mem_prompt, state_lru_conv, state_lru_h, state_conf_conv,
              cache_mem_k, cache_mem_v, g_ff1_pre, ff1_w_gate, ff1_w_up, ff1_w_down, g_ff1_post,
              g_mix_pre, w_in, lru_conv_w, lru_conv_b, lru_w_a, lru_b_a, lru_w_x, lru_b_x, lru_lambda,
              conf_conv_w, conf_conv_b, conf_ln_g, conf_ln_b, w_out, g_mix_post,
              g_mem_pre, g_mem_kv, w_mem_k, w_mem_v, w_q, w_o, g_mem_post,
              g_ff2_pre, ff2_w_gate, ff2_w_up, ff2_w_down, g_ff2_post):
    xp, xs = x_prompt, x_sample
    bp = xp.shape[0]
    lc_p, lh_p, cc_p, mk_p_all, mv_p_all = [], [], [], [], []
    lc_s, lh_s, cc_s = [], [], []
    for l in range(DEPTH):
        p = dict(
            g_ff1_pre=g_ff1_pre[l], ff1_w_gate=ff1_w_gate[l], ff1_w_up=ff1_w_up[l],
            ff1_w_down=ff1_w_down[l], g_ff1_post=g_ff1_post[l],
            g_mix_pre=g_mix_pre[l], w_in=w_in[l], lru_conv_w=lru_conv_w[l], lru_conv_b=lru_conv_b[l],
            lru_w_a=lru_w_a[l], lru_b_a=lru_b_a[l], lru_w_x=lru_w_x[l], lru_b_x=lru_b_x[l],
            lru_lambda=lru_lambda[l], conf_conv_w=conf_conv_w[l], conf_conv_b=conf_conv_b[l],
            conf_ln_g=conf_ln_g[l], conf_ln_b=conf_ln_b[l], w_out=w_out[l], g_mix_post=g_mix_post[l],
            g_mem_pre=g_mem_pre[l], w_q=w_q[l], w_o=w_o[l], g_mem_post=g_mem_post[l],
            g_ff2_pre=g_ff2_pre[l], ff2_w_gate=ff2_w_gate[l], ff2_w_up=ff2_w_up[l],
            ff2_w_down=ff2_w_down[l], g_ff2_post=g_ff2_post[l])
        mk_p, mv_p = memory_kv(mem_prompt, g_mem_kv[l], w_mem_k[l], w_mem_v[l])
        zero_c4 = jnp.zeros((bp, LRU_CONV_W - 1, D_LRU), xp.dtype)
        zero_h = jnp.zeros((bp, D_LRU), xp.dtype)
        zero_c31 = jnp.zeros((bp, CONF_CONV_W - 1, D_CONV), xp.dtype)
        xp, c4p, hp, c31p = decoder_layer(xp, zero_c4, zero_h, zero_c31, mk_p, mv_p, p)
        xs, c4s, hs, c31s = decoder_layer(xs, state_lru_conv[l], state_lru_h[l], state_conf_conv[l],
                                          cache_mem_k[l], cache_mem_v[l], p)
        lc_p.append(c4p); lh_p.append(hp); cc_p.append(c31p)
        mk_p_all.append(mk_p); mv_p_all.append(mv_p)
        lc_s.append(c4s); lh_s.append(hs); cc_s.append(c31s)
    return (xp, xs,
            jnp.stack(lc_p), jnp.stack(lh_p), jnp.stack(cc_p), jnp.stack(mk_p_all), jnp.stack(mv_p_all),
            jnp.stack(lc_s), jnp.stack(lh_s), jnp.stack(cc_s))
```

```python
import functools

import jax
import jax.numpy as jnp
from jax import lax
from jax.experimental import pallas as pl
from jax.experimental.pallas import tpu as pltpu

D = 1024
D_FF = 2816
N_PAIR = 4
PAIR_W = 256
LRU_C = 8.0
C4_W = 4
C31_W = 31
N_MEM = 256
N_HEADS = 4
HEAD_D = 256
FFN_RES = 0.5
EPS = 1e-6
SUB = 8
C31_PREV = 32
VMEM_LIMIT = 56 * 1024 * 1024

BF = jnp.bfloat16
F32 = jnp.float32


def _dot(a, b):
    return jnp.dot(a, b, preferred_element_type=F32)


def _rms(x, g):
    ms = jnp.mean(x * x, axis=-1, keepdims=True)
    return x * lax.rsqrt(ms + EPS) * g


def _sigmoid(x):
    return 1.0 / (1.0 + jnp.exp(-x))


def _gelu_tanh(x):
    return 0.5 * x * (1.0 + jnp.tanh(0.7978845608028654 * (x + 0.044715 * (x * x * x))))


def _ffn(x, gpre, wg_ref, wu_ref, wd_ref, gpost):
    h = _rms(x, gpre).astype(BF)
    gate = _dot(h, wg_ref[...])
    up = _dot(h, wu_ref[...])
    act = (gate * _sigmoid(gate) * up).astype(BF)
    y = _dot(act, wd_ref[...])
    return x + FFN_RES * _rms(y, gpost)


def _lru_gates(xc, wax_ref, ba, bx, lam):
    xcb = xc.astype(BF)
    r_parts, i_parts = [], []
    for p in range(N_PAIR):
        g2 = _dot(xcb[:, p * PAIR_W:(p + 1) * PAIR_W], wax_ref[p])
        r_parts.append(g2[:, :PAIR_W])
        i_parts.append(g2[:, PAIR_W:])
    r = _sigmoid(jnp.concatenate(r_parts, axis=-1) + ba)
    i = _sigmoid(jnp.concatenate(i_parts, axis=-1) + bx)
    softplus_neg_lam = jnp.maximum(-lam, 0.0) + jnp.log1p(jnp.exp(-jnp.abs(lam)))
    log_a = (-LRU_C * r) * softplus_neg_lam
    a = jnp.exp(log_a)
    th = jnp.tanh(log_a)
    one_minus_a2 = (-2.0 * th) / (1.0 - th)
    u = jnp.sqrt(one_minus_a2) * (i * xc)
    return a, u


def _group_scan(a3, u3):
    row = lax.broadcasted_iota(jnp.int32, (1, SUB, D), 1)
    for s in (1, 2, 4):
        m = row >= s
        a_sh = pltpu.roll(a3, s, axis=1)
        u_sh = pltpu.roll(u3, s, axis=1)
        u3 = jnp.where(m, a3 * u_sh + u3, u3)
        a3 = jnp.where(m, a3 * a_sh, a3)
    return a3, u3


def _layernorm(x, g, b):
    xc = x - jnp.mean(x, axis=-1, keepdims=True)
    return xc * lax.rsqrt(jnp.mean(xc * xc, axis=-1, keepdims=True) + EPS) * g + b


def _mix_out(x, z_g, lru, c, lng, lnb, wout_ref, gpost):
    y_lru = (_gelu_tanh(z_g) * lru).astype(BF)
    yc = _layernorm(c, lng, lnb)
    y_conf = (yc * _sigmoid(yc)).astype(BF)
    out = _dot(y_lru, wout_ref[pl.ds(0, D), :]) + _dot(y_conf, wout_ref[pl.ds(D, D), :])
    return x + _rms(out, gpost)


def _ffn_kernel(x_ref, gpre_ref, wg_ref, wu_ref, wd_ref, gpost_ref, o_ref):
    o_ref[...] = _ffn(x_ref[...], gpre_ref[...], wg_ref, wu_ref, wd_ref, gpost_ref[...])


def _const_spec(shape):
    n = len(shape)
    return pl.BlockSpec(shape, lambda *_: (0,) * n, pipeline_mode=pl.Buffered(1))


def _ffn_call(x2d, gpre, wg, wu, wd, gpost, tm):
    rows = x2d.shape[0]
    return pl.pallas_call(
        _ffn_kernel,
        grid=(rows // tm,),
        in_specs=[pl.BlockSpec((tm, D), lambda i: (i, 0)),
                  _const_spec((1, D)), _const_spec((D, D_FF)), _const_spec((D, D_FF)),
                  _const_spec((D_FF, D)), _const_spec((1, D))],
        out_specs=pl.BlockSpec((tm, D), lambda i: (i, 0)),
        out_shape=jax.ShapeDtypeStruct((rows, D), F32),
        compiler_params=pltpu.CompilerParams(
            dimension_semantics=("arbitrary",), vmem_limit_bytes=VMEM_LIMIT),
        name="ffn",
    )(x2d, gpre, wg, wu, wd, gpost)


def _mixer_prompt_kernel(x_ref, gpre_ref, win_ref, c4w_ref, c4b_ref, wax_ref, ba_ref, bx_ref,
                         lam_ref, c31w_ref, c31b_ref, lng_ref, lnb_ref, wout_ref, gpost_ref,
                         o_ref, nc4_ref, nh_ref, nc31_ref,
                         zx_buf, glu_buf, sh_buf, a_buf, u_buf, c_buf, h_carry, *, tm):
    t = pl.program_id(1)
    n_t = pl.num_programs(1)

    @pl.when(t == 0)
    def _():
        zx_buf[pl.ds(0, SUB), :] = jnp.zeros((SUB, D), F32)
        glu_buf[pl.ds(0, C31_PREV), :] = jnp.zeros((C31_PREV, D), F32)
        h_carry[...] = jnp.zeros((SUB, D), F32)

    x = x_ref[...]
    h = _rms(x, gpre_ref[...]).astype(BF)

    zx_buf[pl.ds(SUB, tm), :] = _dot(h, win_ref[:, pl.ds(0, D)])
    xc = jnp.broadcast_to(c4b_ref[...], (tm, D))
    for k in range(C4_W):
        xc = xc + c4w_ref[pl.ds(k, 1), :] * zx_buf[pl.ds(SUB - (C4_W - 1) + k, tm), :]

    @pl.when(t == n_t - 1)
    def _():
        nc4_ref[...] = zx_buf[pl.ds(SUB + tm - (C4_W - 1), C4_W - 1), :]

    zx_buf[pl.ds(0, SUB), :] = zx_buf[pl.ds(tm, SUB), :]

    a, u = _lru_gates(xc, wax_ref, ba_ref[...], bx_ref[...], lam_ref[...])
    groups = tm // SUB
    a3, u3 = _group_scan(a.reshape(groups, SUB, D), u.reshape(groups, SUB, D))
    a_buf[...] = a3.reshape(tm, D)
    u_buf[...] = u3.reshape(tm, D)

    def carry_body(g, hc):
        off = pl.multiple_of(g * SUB, SUB)
        hg = a_buf[pl.ds(off, SUB), :] * hc + u_buf[pl.ds(off, SUB), :]
        u_buf[pl.ds(off, SUB), :] = hg
        return jnp.broadcast_to(hg[SUB - 1:SUB, :], (SUB, D))

    hc = lax.fori_loop(0, groups, carry_body, h_carry[...])
    h_carry[...] = hc
    nh_ref[...] = hc[0:1, :]
    lru = u_buf[...]

    z_v = _dot(h, win_ref[:, pl.ds(2 * D, D)])
    z_gate = _dot(h, win_ref[:, pl.ds(3 * D, D)])
    glu_buf[pl.ds(C31_PREV, tm), :] = z_v * _sigmoid(z_gate)
    for r in range(1, SUB):
        sh_buf[r - 1] = glu_buf[pl.ds(r, tm + C31_PREV - SUB), :]

    rb = 32
    first = C31_PREV - (C31_W - 1)

    def conv_body(ci, carry):
        t0 = pl.multiple_of(ci * rb, rb)
        acc = jnp.broadcast_to(c31b_ref[...], (rb, D))
        for k in range(C31_W):
            q, r = divmod(first + k, SUB)
            if r == 0:
                src = glu_buf[pl.ds(t0 + SUB * q, rb), :]
            else:
                src = sh_buf[r - 1, pl.ds(t0 + SUB * q, rb), :]
            acc = acc + c31w_ref[pl.ds(k, 1), :] * src
        c_buf[pl.ds(t0, rb), :] = acc
        return carry

    lax.fori_loop(0, tm // rb, conv_body, 0)

    @pl.when(t == n_t - 1)
    def _():
        nc31_ref[...] = glu_buf[pl.ds(C31_PREV + tm - (C31_W - 1), C31_W - 1), :]

    glu_buf[pl.ds(0, C31_PREV), :] = glu_buf[pl.ds(tm, C31_PREV), :]

    z_g = _dot(h, win_ref[:, pl.ds(D, D)])
    o_ref[...] = _mix_out(x, z_g, lru, c_buf[...], lng_ref[...], lnb_ref[...], wout_ref,
                          gpost_ref[...])


def _mixer_prompt_call(x3d, gpre, win, c4w, c4b, wax, ba, bx, lam, c31w, c31b, lng, lnb, wout,
                       gpost, tm):
    b, s, _ = x3d.shape
    vec = _const_spec((1, D))
    return pl.pallas_call(
        functools.partial(_mixer_prompt_kernel, tm=tm),
        grid=(b, s // tm),
        in_specs=[pl.BlockSpec((None, tm, D), lambda i, j: (i, j, 0)),
                  vec, _const_spec((D, 4 * D)), _const_spec((C4_W, D)), vec,
                  _const_spec((N_PAIR, PAIR_W, 2 * PAIR_W)), vec, vec, vec,
                  _const_spec((C31_W, D)), vec, vec, vec, _const_spec((2 * D, D)), vec],
        out_specs=[pl.BlockSpec((None, tm, D), lambda i, j: (i, j, 0)),
                   pl.BlockSpec((None, C4_W - 1, D), lambda i, j: (i, 0, 0)),
                   pl.BlockSpec((None, 1, D), lambda i, j: (i, 0, 0)),
                   pl.BlockSpec((None, C31_W - 1, D), lambda i, j: (i, 0, 0))],
        out_shape=[jax.ShapeDtypeStruct((b, s, D), F32),
                   jax.ShapeDtypeStruct((b, C4_W - 1, D), F32),
                   jax.ShapeDtypeStruct((b, 1, D), F32),
                   jax.ShapeDtypeStruct((b, C31_W - 1, D), F32)],
        scratch_shapes=[pltpu.VMEM((tm + SUB, D), F32),
                        pltpu.VMEM((tm + C31_PREV, D), F32),
                        pltpu.VMEM((SUB - 1, tm + C31_PREV - SUB, D), F32),
                        pltpu.VMEM((tm, D), F32),
                        pltpu.VMEM((tm, D), F32),
                        pltpu.VMEM((tm, D), F32),
                        pltpu.VMEM((SUB, D), F32)],
        compiler_params=pltpu.CompilerParams(
            dimension_semantics=("arbitrary", "arbitrary"), vmem_limit_bytes=VMEM_LIMIT),
        name="mixer_prompt",
    )(x3d, gpre, win, c4w, c4b, wax, ba, bx, lam, c31w, c31b, lng, lnb, wout, gpost)


def _mixer_sample_kernel(x_ref, c4s_ref, h0_ref, c31s_ref, gpre_ref, win_ref, c4w_ref, c4b_ref,
                         wax_ref, ba_ref, bx_ref, lam_ref, c31w_ref, c31b_ref, lng_ref, lnb_ref,
                         wout_ref, gpost_ref, gq_ref, wq_ref,
                         o_ref, q_ref, nc4_ref, nh_ref, nc31_ref,
                         zx_buf, glu_buf, c_buf, *, sb):
    rows = sb * SUB
    x = x_ref[...]
    h = _rms(x, gpre_ref[...]).astype(BF)

    zx_buf[:, pl.ds(SUB - (C4_W - 1), C4_W - 1), :] = c4s_ref[...]
    zx_buf[:, pl.ds(SUB, SUB), :] = _dot(h, win_ref[:, pl.ds(0, D)]).reshape(sb, SUB, D)
    xc = jnp.broadcast_to(c4b_ref[...].reshape(1, 1, D), (sb, SUB, D))
    for k in range(C4_W):
        xc = xc + c4w_ref[pl.ds(k, 1), :].reshape(1, 1, D) * zx_buf[:, pl.ds(SUB - (C4_W - 1) + k, SUB), :]
    nc4_ref[...] = zx_buf[:, pl.ds(2 * SUB - (C4_W - 1), C4_W - 1), :]

    a, u = _lru_gates(xc.reshape(rows, D), wax_ref, ba_ref[...], bx_ref[...], lam_ref[...])
    a3, u3 = _group_scan(a.reshape(sb, SUB, D), u.reshape(sb, SUB, D))
    lru3 = a3 * h0_ref[...] + u3
    nh_ref[...] = lru3[:, SUB - 1:SUB, :]

    z_v = _dot(h, win_ref[:, pl.ds(2 * D, D)])
    z_gate = _dot(h, win_ref[:, pl.ds(3 * D, D)])
    hist = C31_W - 1
    glu_buf[:, pl.ds(0, hist), :] = c31s_ref[...]
    glu_buf[:, pl.ds(hist, SUB), :] = (z_v * _sigmoid(z_gate)).reshape(sb, SUB, D)
    nc31_ref[...] = glu_buf[:, pl.ds(SUB, hist), :]

    sc = 4

    def conv_body(ci, carry):
        s0 = pl.multiple_of(ci * sc, sc)
        acc = jnp.broadcast_to(c31b_ref[...].reshape(1, 1, D), (sc, SUB, D))
        for k in range(C31_W):
            acc = acc + c31w_ref[pl.ds(k, 1), :].reshape(1, 1, D) * glu_buf[pl.ds(s0, sc), pl.ds(k, SUB), :]
        c_buf[pl.ds(s0, sc)] = acc
        return carry

    lax.fori_loop(0, sb // sc, conv_body, 0)

    z_g = _dot(h, win_ref[:, pl.ds(D, D)])
    x2 = _mix_out(x, z_g, lru3.reshape(rows, D), c_buf[...].reshape(rows, D), lng_ref[...],
                  lnb_ref[...], wout_ref, gpost_ref[...])
    o_ref[...] = x2
    q_ref[...] = _dot(_rms(x2, gq_ref[...]).astype(BF), wq_ref[...])


def _mixer_sample_call(x2d, c4s, h0, c31s, gpre, win, c4w, c4b, wax, ba, bx, lam, c31w, c31b, lng,
                       lnb, wout, gpost, gq, wq, sb):
    n_seq = c4s.shape[0]
    rows = sb * SUB
    vec = _const_spec((1, D))
    return pl.pallas_call(
        functools.partial(_mixer_sample_kernel, sb=sb),
        grid=(n_seq // sb,),
        in_specs=[pl.BlockSpec((rows, D), lambda i: (i, 0)),
                  pl.BlockSpec((sb, C4_W - 1, D), lambda i: (i, 0, 0)),
                  pl.BlockSpec((sb, 1, D), lambda i: (i, 0, 0)),
                  pl.BlockSpec((sb, C31_W - 1, D), lambda i: (i, 0, 0)),
                  vec, _const_spec((D, 4 * D)), _const_spec((C4_W, D)), vec,
                  _const_spec((N_PAIR, PAIR_W, 2 * PAIR_W)), vec, vec, vec,
                  _const_spec((C31_W, D)), vec, vec, vec, _const_spec((2 * D, D)), vec,
                  vec, _const_spec((D, D))],
        out_specs=[pl.BlockSpec((rows, D), lambda i: (i, 0)),
                   pl.BlockSpec((rows, D), lambda i: (i, 0)),
                   pl.BlockSpec((sb, C4_W - 1, D), lambda i: (i, 0, 0)),
                   pl.BlockSpec((sb, 1, D), lambda i: (i, 0, 0)),
                   pl.BlockSpec((sb, C31_W - 1, D), lambda i: (i, 0, 0))],
        out_shape=[jax.ShapeDtypeStruct((n_seq * SUB, D), F32),
                   jax.ShapeDtypeStruct((n_seq * SUB, D), F32),
                   jax.ShapeDtypeStruct((n_seq, C4_W - 1, D), F32),
                   jax.ShapeDtypeStruct((n_seq, 1, D), F32),
                   jax.ShapeDtypeStruct((n_seq, C31_W - 1, D), F32)],
        scratch_shapes=[pltpu.VMEM((sb, 2 * SUB, D), F32),
                        pltpu.VMEM((sb, C31_W - 1 + SUB + 2, D), F32),
                        pltpu.VMEM((sb, SUB, D), F32)],
        compiler_params=pltpu.CompilerParams(
            dimension_semantics=("arbitrary",), vmem_limit_bytes=VMEM_LIMIT),
        name="mixer_sample",
    )(x2d, c4s, h0, c31s, gpre, win, c4w, c4b, wax, ba, bx, lam, c31w, c31b, lng, lnb, wout,
      gpost, gq, wq)


def _memkv_kernel(m_ref, g_ref, wk_ref, wv_ref, k_ref, v_ref):
    m = _rms(m_ref[...], g_ref[...]).astype(BF)
    k_ref[...] = _dot(m, wk_ref[...])
    v_ref[...] = _dot(m, wv_ref[...])


def _memkv_call(mem2d, g, wk, wv, tm):
    rows = mem2d.shape[0]
    return pl.pallas_call(
        _memkv_kernel,
        grid=(rows // tm,),
        in_specs=[pl.BlockSpec((tm, D), lambda i: (i, 0)),
                  _const_spec((1, D)), _const_spec((D, D)), _const_spec((D, D))],
        out_specs=[pl.BlockSpec((tm, D), lambda i: (i, 0)),
                   pl.BlockSpec((tm, D), lambda i: (i, 0))],
        out_shape=[jax.ShapeDtypeStruct((rows, D), F32), jax.ShapeDtypeStruct((rows, D), F32)],
        compiler_params=pltpu.CompilerParams(
            dimension_semantics=("arbitrary",), vmem_limit_bytes=VMEM_LIMIT),
        name="memkv",
    )(mem2d, g, wk, wv)


def _attend(q, k, v):
    outs = []
    for hh in range(N_HEADS):
        sl = slice(hh * HEAD_D, (hh + 1) * HEAD_D)
        s = lax.dot_general(q[:, sl].astype(BF), k[:, sl], (((1,), (1,)), ((), ())),
                            preferred_element_type=F32) * (HEAD_D ** -0.5)
        e = jnp.exp(s - jnp.max(s, axis=-1, keepdims=True))
        p = e / jnp.sum(e, axis=-1, keepdims=True)
        outs.append(_dot(p.astype(BF), v[:, sl]))
    return jnp.concatenate(outs, axis=-1)


def _attn_ffn_prompt_kernel(x_ref, k_ref, v_ref, gq_ref, wq_ref, wo_ref, gao_ref,
                            gpre_ref, wg_ref, wu_ref, wd_ref, gpost_ref, o_ref):
    x = x_ref[...]
    q = _dot(_rms(x, gq_ref[...]).astype(BF), wq_ref[...])
    o = _attend(q, k_ref[...].astype(BF), v_ref[...].astype(BF))
    x3 = x + _rms(_dot(o.astype(BF), wo_ref[...]), gao_ref[...])
    o_ref[...] = _ffn(x3, gpre_ref[...], wg_ref, wu_ref, wd_ref, gpost_ref[...])


def _attn_ffn_prompt_call(x3d, k3d, v3d, gq, wq, wo, gao, gpre, wg, wu, wd, gpost, tm):
    b, s, _ = x3d.shape
    vec = _const_spec((1, D))
    return pl.pallas_call(
        _attn_ffn_prompt_kernel,
        grid=(b, s // tm),
        in_specs=[pl.BlockSpec((None, tm, D), lambda i, j: (i, j, 0)),
                  pl.BlockSpec((None, N_MEM, D), lambda i, j: (i, 0, 0)),
                  pl.BlockSpec((None, N_MEM, D), lambda i, j: (i, 0, 0)),
                  vec, _const_spec((D, D)), _const_spec((D, D)), vec,
                  vec, _const_spec((D, D_FF)), _const_spec((D, D_FF)), _const_spec((D_FF, D)), vec],
        out_specs=pl.BlockSpec((None, tm, D), lambda i, j: (i, j, 0)),
        out_shape=jax.ShapeDtypeStruct((b, s, D), F32),
        compiler_params=pltpu.CompilerParams(
            dimension_semantics=("arbitrary", "arbitrary"), vmem_limit_bytes=VMEM_LIMIT),
        name="attn_ffn_prompt",
    )(x3d, k3d, v3d, gq, wq, wo, gao, gpre, wg, wu, wd, gpost)


def _attn_sample_kernel(q_ref, k_ref, v_ref, o_ref, *, sq):
    for s in range(sq):
        q = q_ref[pl.ds(s * SUB, SUB), :]
        o_ref[pl.ds(s * SUB, SUB), :] = _attend(q, k_ref[s].astype(BF), v_ref[s].astype(BF))


def _attn_sample_call(q2d, k3d, v3d, sq):
    n_seq = k3d.shape[0]
    rows = sq * SUB
    return pl.pallas_call(
        functools.partial(_attn_sample_kernel, sq=sq),
        grid=(n_seq // sq,),
        in_specs=[pl.BlockSpec((rows, D), lambda i: (i, 0)),
                  pl.BlockSpec((sq, N_MEM, D), lambda i: (i, 0, 0)),
                  pl.BlockSpec((sq, N_MEM, D), lambda i: (i, 0, 0))],
        out_specs=pl.BlockSpec((rows, D), lambda i: (i, 0)),
        out_shape=jax.ShapeDtypeStruct((n_seq * SUB, D), F32),
        compiler_params=pltpu.CompilerParams(
            dimension_semantics=("arbitrary",), vmem_limit_bytes=VMEM_LIMIT),
        name="attn_sample",
    )(q2d, k3d, v3d)


def _oproj_ffn_kernel(x_ref, a_ref, wo_ref, gao_ref, gpre_ref, wg_ref, wu_ref, wd_ref, gpost_ref,
                      o_ref):
    x3 = x_ref[...] + _rms(_dot(a_ref[...].astype(BF), wo_ref[...]), gao_ref[...])
    o_ref[...] = _ffn(x3, gpre_ref[...], wg_ref, wu_ref, wd_ref, gpost_ref[...])


def _oproj_ffn_call(x2d, a2d, wo, gao, gpre, wg, wu, wd, gpost, tm):
    rows = x2d.shape[0]
    vec = _const_spec((1, D))
    return pl.pallas_call(
        _oproj_ffn_kernel,
        grid=(rows // tm,),
        in_specs=[pl.BlockSpec((tm, D), lambda i: (i, 0)), pl.BlockSpec((tm, D), lambda i: (i, 0)),
                  _const_spec((D, D)), vec,
                  vec, _const_spec((D, D_FF)), _const_spec((D, D_FF)), _const_spec((D_FF, D)), vec],
        out_specs=pl.BlockSpec((tm, D), lambda i: (i, 0)),
        out_shape=jax.ShapeDtypeStruct((rows, D), F32),
        compiler_params=pltpu.CompilerParams(
            dimension_semantics=("arbitrary",), vmem_limit_bytes=VMEM_LIMIT),
        name="oproj_ffn",
    )(x2d, a2d, wo, gao, gpre, wg, wu, wd, gpost)


def _pair_gate_weights(w_a, w_x):
    hd = w_a.shape[-1]

    def blockdiag(w):
        w = w.reshape(N_PAIR, 2, hd, hd)
        z = jnp.zeros((N_PAIR, hd, hd), w.dtype)
        top = jnp.concatenate([w[:, 0], z], axis=-1)
        bot = jnp.concatenate([z, w[:, 1]], axis=-1)
        return jnp.concatenate([top, bot], axis=-2)

    return jnp.concatenate([blockdiag(w_a), blockdiag(w_x)], axis=-1).astype(BF)


def _layer(l, xp, xs, mem_prompt, state_lru_conv, state_lru_h, state_conf_conv, cache_mem_k,
           cache_mem_v, w):
    bp, sp, _ = xp.shape
    bs, ss, _ = xs.shape
    vec = lambda name: w[name][l].reshape(1, D)
    bf = lambda name: w[name][l].astype(BF)

    ff1 = (vec('g_ff1_pre'), bf('ff1_w_gate'), bf('ff1_w_up'), bf('ff1_w_down'), vec('g_ff1_post'))
    ff2 = (vec('g_ff2_pre'), bf('ff2_w_gate'), bf('ff2_w_up'), bf('ff2_w_down'), vec('g_ff2_post'))
    mix = (vec('g_mix_pre'), bf('w_in'), w['lru_conv_w'][l], vec('lru_conv_b'),
           _pair_gate_weights(w['lru_w_a'][l], w['lru_w_x'][l]), vec('lru_b_a'), vec('lru_b_x'),
           vec('lru_lambda'), w['conf_conv_w'][l], vec('conf_conv_b'), vec('conf_ln_g'),
           vec('conf_ln_b'), bf('w_out'), vec('g_mix_post'))
    wq, wo = bf('w_q'), bf('w_o')

    mk, mv = _memkv_call(mem_prompt.reshape(bp * N_MEM, D), vec('g_mem_kv'), bf('w_mem_k'),
                         bf('w_mem_v'), tm=512)
    x1 = _ffn_call(xp.reshape(bp * sp, D), *ff1, tm=512).reshape(bp, sp, D)
    x2, c4p, hp, c31p = _mixer_prompt_call(x1, *mix, tm=256)
    xp_out = _attn_ffn_prompt_call(x2, mk.reshape(bp, N_MEM, D), mv.reshape(bp, N_MEM, D),
                                   vec('g_mem_pre'), wq, wo, vec('g_mem_post'), *ff2, tm=512)

    s1 = _ffn_call(xs.reshape(bs * ss, D), *ff1, tm=512)
    s2, q, c4s, hs, c31s = _mixer_sample_call(
        s1, state_lru_conv[l], state_lru_h[l].reshape(bs, 1, D), state_conf_conv[l], *mix,
        vec('g_mem_pre'), wq, sb=32)
    att = _attn_sample_call(q, cache_mem_k[l].reshape(bs, N_MEM, D),
                            cache_mem_v[l].reshape(bs, N_MEM, D), sq=8)
    xs_out = _oproj_ffn_call(s2, att, wo, vec('g_mem_post'), *ff2, tm=512).reshape(bs, ss, D)

    kv_shape = (bp, N_MEM, N_HEADS, HEAD_D)
    return (xp_out, xs_out, c4p, hp.reshape(bp, D), c31p, mk.reshape(kv_shape), mv.reshape(kv_shape),
            c4s, hs.reshape(bs, D), c31s)


def kernel(x_prompt, x_sample, mem_prompt, state_lru_conv, state_lru_h, state_conf_conv, cache_mem_k, cache_mem_v, g_ff1_pre, ff1_w_gate, ff1_w_up, ff1_w_down, g_ff1_post, g_mix_pre, w_in, lru_conv_w, lru_conv_b, lru_w_a, lru_b_a, lru_w_x, lru_b_x, lru_lambda, conf_conv_w, conf_conv_b, conf_ln_g, conf_ln_b, w_out, g_mix_post, g_mem_pre, g_mem_kv, w_mem_k, w_mem_v, w_q, w_o, g_mem_post, g_ff2_pre, ff2_w_gate, ff2_w_up, ff2_w_down, g_ff2_post):
    w = dict(g_ff1_pre=g_ff1_pre, ff1_w_gate=ff1_w_gate, ff1_w_up=ff1_w_up, ff1_w_down=ff1_w_down,
             g_ff1_post=g_ff1_post, g_mix_pre=g_mix_pre, w_in=w_in, lru_conv_w=lru_conv_w,
             lru_conv_b=lru_conv_b, lru_w_a=lru_w_a, lru_b_a=lru_b_a, lru_w_x=lru_w_x,
             lru_b_x=lru_b_x, lru_lambda=lru_lambda, conf_conv_w=conf_conv_w,
             conf_conv_b=conf_conv_b, conf_ln_g=conf_ln_g, conf_ln_b=conf_ln_b, w_out=w_out,
             g_mix_post=g_mix_post, g_mem_pre=g_mem_pre, g_mem_kv=g_mem_kv, w_mem_k=w_mem_k,
             w_mem_v=w_mem_v, w_q=w_q, w_o=w_o, g_mem_post=g_mem_post, g_ff2_pre=g_ff2_pre,
             ff2_w_gate=ff2_w_gate, ff2_w_up=ff2_w_up, ff2_w_down=ff2_w_down, g_ff2_post=g_ff2_post)
    depth = w_in.shape[0]
    xp, xs = x_prompt, x_sample
    per_layer = []
    for l in range(depth):
        outs = _layer(l, xp, xs, mem_prompt, state_lru_conv, state_lru_h, state_conf_conv,
                      cache_mem_k, cache_mem_v, w)
        xp, xs = outs[0], outs[1]
        per_layer.append(outs[2:])
    stacked = [jnp.stack([pl_[i] for pl_ in per_layer]) for i in range(8)]
    return (xp, xs, *stacked)
```

```python
import functools

import jax
import jax.numpy as jnp
from jax import lax
from jax.experimental import pallas as pl
from jax.experimental.pallas import tpu as pltpu

D = 1024
D_FF = 2816
N_PAIR = 4
PAIR_W = 256
LRU_C = 8.0
C4_W = 4
C31_W = 31
N_MEM = 256
N_HEADS = 4
HEAD_D = 256
FFN_RES = 0.5
EPS = 1e-6
SUB = 8
CONV_ACC_VREGS = 32
VMEM_LIMIT = 56 * 1024 * 1024

BF = jnp.bfloat16
F32 = jnp.float32


def _dot(a, b):
    return jnp.dot(a, b, preferred_element_type=F32)


def _rms(x, g):
    ms = jnp.mean(x * x, axis=-1, keepdims=True)
    return x * lax.rsqrt(ms + EPS) * g


def _sigmoid(x):
    return 1.0 / (1.0 + jnp.exp(-x))


def _gelu_tanh(x):
    return 0.5 * x * (1.0 + jnp.tanh(0.7978845608028654 * (x + 0.044715 * (x * x * x))))


def _ffn(x, gpre, wg_ref, wu_ref, wd_ref, gpost):
    h = _rms(x, gpre).astype(BF)
    gate = _dot(h, wg_ref[...])
    up = _dot(h, wu_ref[...])
    act = (gate * _sigmoid(gate) * up).astype(BF)
    y = _dot(act, wd_ref[...])
    return x + FFN_RES * _rms(y, gpost)


def _lru_gates(xc, wax_ref, ba, bx, lam):
    xcb = xc.astype(BF)
    r_parts, i_parts = [], []
    for p in range(N_PAIR):
        g2 = _dot(xcb[:, p * PAIR_W:(p + 1) * PAIR_W], wax_ref[p])
        r_parts.append(g2[:, :PAIR_W])
        i_parts.append(g2[:, PAIR_W:])
    r = _sigmoid(jnp.concatenate(r_parts, axis=-1) + ba)
    i = _sigmoid(jnp.concatenate(i_parts, axis=-1) + bx)
    softplus_neg_lam = jnp.maximum(-lam, 0.0) + jnp.log1p(jnp.exp(-jnp.abs(lam)))
    log_a = (-LRU_C * r) * softplus_neg_lam
    a = jnp.exp(log_a)
    th = jnp.tanh(log_a)
    one_minus_a2 = (-2.0 * th) / (1.0 - th)
    u = jnp.sqrt(one_minus_a2) * (i * xc)
    return a, u


def _layernorm(x, g, b):
    xc = x - jnp.mean(x, axis=-1, keepdims=True)
    return xc * lax.rsqrt(jnp.mean(xc * xc, axis=-1, keepdims=True) + EPS) * g + b


def _const_spec(shape):
    n = len(shape)
    return pl.BlockSpec(shape, lambda *_: (0,) * n, pipeline_mode=pl.Buffered(1))


def _params(n_grid):
    return pltpu.CompilerParams(dimension_semantics=("arbitrary",) * n_grid,
                                vmem_limit_bytes=VMEM_LIMIT)


def _ffn_tmajor_kernel(x_ref, gpre_ref, wg_ref, wu_ref, wd_ref, gpost_ref, o_ref, *, nsb, tt):
    y = _ffn(x_ref[...].reshape(nsb * tt, D), gpre_ref[...], wg_ref, wu_ref, wd_ref, gpost_ref[...])
    o_ref[...] = jnp.swapaxes(y.reshape(nsb, tt, D), 0, 1)


def _ffn_tmajor_call(x3d, gpre, wg, wu, wd, gpost, nsb, tt):
    ns, t, _ = x3d.shape
    return pl.pallas_call(
        functools.partial(_ffn_tmajor_kernel, nsb=nsb, tt=tt),
        grid=(ns // nsb, t // tt),
        in_specs=[pl.BlockSpec((nsb, tt, D), lambda i, j: (i, j, 0)),
                  _const_spec((1, D)), _const_spec((D, D_FF)), _const_spec((D, D_FF)),
                  _const_spec((D_FF, D)), _const_spec((1, D))],
        out_specs=pl.BlockSpec((tt, nsb, D), lambda i, j: (j, i, 0)),
        out_shape=jax.ShapeDtypeStruct((t, ns, D), F32),
        compiler_params=_params(2),
        name="ffn1",
    )(x3d, gpre, wg, wu, wd, gpost)


def _mixer_kernel(*refs, nsb, tt, with_q):
    (x_ref, h4_ref, h0_ref, h31_ref, gpre_ref, win_ref, c4w_ref, c4b_ref, wax_ref, ba_ref, bx_ref,
     lam_ref, c31w_ref, c31b_ref, lng_ref, lnb_ref, wout_ref, gpost_ref) = refs[:18]
    refs = refs[18:]
    if with_q:
        gq_ref, wq_ref, o_ref, q_ref = refs[:4]
        refs = refs[4:]
    else:
        o_ref = refs[0]
        refs = refs[1:]
    nc4_ref, nh_ref, nc31_ref, zb, cb, a_buf, u_buf, c_buf, h_carry, w4b, w31b = refs

    g = nsb // SUB
    rows = tt * nsb
    hist4, hist31 = C4_W - 1, C31_W - 1
    j = pl.program_id(1)
    last = j == pl.num_programs(1) - 1

    @pl.when((pl.program_id(0) == 0) & (j == 0))
    def _():
        for k in range(C4_W):
            w4b[k] = jnp.broadcast_to(c4w_ref[pl.ds(k, 1), :], (SUB, D))
        for k in range(C31_W):
            w31b[k] = jnp.broadcast_to(c31w_ref[pl.ds(k, 1), :], (SUB, D))

    @pl.when(j == 0)
    def _():
        zb[pl.ds(0, hist4)] = h4_ref[...].reshape(hist4, g, SUB, D)
        cb[pl.ds(0, hist31)] = h31_ref[...].reshape(hist31, g, SUB, D)
        h_carry[...] = h0_ref[...].reshape(g, SUB, D)

    x = x_ref[...].reshape(rows, D)
    h = _rms(x, gpre_ref[...]).astype(BF)

    zb[pl.ds(hist4, tt)] = _dot(h, win_ref[:, pl.ds(0, D)]).reshape(tt, g, SUB, D)
    xc = jnp.broadcast_to(c4b_ref[...].reshape(1, 1, 1, D), (tt, g, SUB, D))
    for k in range(C4_W):
        xc = xc + w4b[k][None, None] * zb[pl.ds(k, tt)]

    @pl.when(last)
    def _():
        nc4_ref[...] = zb[pl.ds(tt, hist4)].reshape(hist4, nsb, D)

    zb[pl.ds(0, hist4)] = zb[pl.ds(tt, hist4)]

    a, u = _lru_gates(xc.reshape(rows, D), wax_ref, ba_ref[...], bx_ref[...], lam_ref[...])
    a_buf[...] = a.reshape(tt, g, SUB, D)
    u_buf[...] = u.reshape(tt, g, SUB, D)

    def scan_body(t, hc):
        hc = a_buf[t] * hc + u_buf[t]
        u_buf[t] = hc
        return hc

    hc = lax.fori_loop(0, tt, scan_body, h_carry[...])
    h_carry[...] = hc

    @pl.when(last)
    def _():
        nh_ref[...] = hc.reshape(nsb, D)

    z_v = _dot(h, win_ref[:, pl.ds(2 * D, D)])
    z_gate = _dot(h, win_ref[:, pl.ds(3 * D, D)])
    cb[pl.ds(hist31, tt)] = (z_v * _sigmoid(z_gate)).reshape(tt, g, SUB, D)

    rb = max(1, CONV_ACC_VREGS // nsb)
    bias = jnp.broadcast_to(c31b_ref[...].reshape(1, 1, D), (g, SUB, D))

    def conv_body(ci, carry):
        t0 = ci * rb
        acc = [bias for _ in range(rb)]
        for s in range(rb + hist31):
            xs = cb[t0 + s]
            for r in range(rb):
                k = s - r
                if 0 <= k < C31_W:
                    acc[r] = acc[r] + w31b[k][None] * xs
        for r in range(rb):
            c_buf[t0 + r] = acc[r]
        return carry

    lax.fori_loop(0, tt // rb, conv_body, 0)

    @pl.when(last)
    def _():
        nc31_ref[...] = cb[pl.ds(tt, hist31)].reshape(hist31, nsb, D)

    for s in range(hist31):
        cb[s] = cb[tt + s]

    z_g = _dot(h, win_ref[:, pl.ds(D, D)])
    y_lru = (_gelu_tanh(z_g) * u_buf[...].reshape(rows, D)).astype(BF)
    yc = _layernorm(c_buf[...].reshape(rows, D), lng_ref[...], lnb_ref[...])
    y_conf = (yc * _sigmoid(yc)).astype(BF)
    out = _dot(y_lru, wout_ref[pl.ds(0, D), :]) + _dot(y_conf, wout_ref[pl.ds(D, D), :])
    x2 = x + _rms(out, gpost_ref[...])
    o_ref[...] = jnp.swapaxes(x2.reshape(tt, nsb, D), 0, 1)
    if with_q:
        q = _dot(_rms(x2, gq_ref[...]).astype(BF), wq_ref[...])
        q_ref[...] = jnp.swapaxes(q.reshape(tt, nsb, D), 0, 1)


def _mixer_call(x_tm, hist4, h0, hist31, mix, q_params, nsb, tt):
    t, ns, _ = x_tm.shape
    with_q = q_params is not None
    vec = _const_spec((1, D))
    seq_major = pl.BlockSpec((nsb, tt, D), lambda i, j: (i, j, 0))
    in_specs = [pl.BlockSpec((tt, nsb, D), lambda i, j: (j, i, 0)),
                pl.BlockSpec((C4_W - 1, nsb, D), lambda i, j: (0, i, 0)),
                pl.BlockSpec((nsb, D), lambda i, j: (i, 0)),
                pl.BlockSpec((C31_W - 1, nsb, D), lambda i, j: (0, i, 0)),
                vec, _const_spec((D, 4 * D)), _const_spec((C4_W, D)), vec,
                _const_spec((N_PAIR, PAIR_W, 2 * PAIR_W)), vec, vec, vec,
                _const_spec((C31_W, D)), vec, vec, vec, _const_spec((2 * D, D)), vec]
    out_specs = [seq_major]
    out_shape = [jax.ShapeDtypeStruct((ns, t, D), F32)]
    args = [x_tm, hist4, h0, hist31, *mix]
    if with_q:
        in_specs += [vec, _const_spec((D, D))]
        out_specs.append(seq_major)
        out_shape.append(jax.ShapeDtypeStruct((ns, t, D), F32))
        args += list(q_params)
    out_specs += [pl.BlockSpec((C4_W - 1, nsb, D), lambda i, j: (0, i, 0)),
                  pl.BlockSpec((nsb, D), lambda i, j: (i, 0)),
                  pl.BlockSpec((C31_W - 1, nsb, D), lambda i, j: (0, i, 0))]
    out_shape += [jax.ShapeDtypeStruct((C4_W - 1, ns, D), F32),
                  jax.ShapeDtypeStruct((ns, D), F32),
                  jax.ShapeDtypeStruct((C31_W - 1, ns, D), F32)]
    g = nsb // SUB
    tile = lambda n: pltpu.VMEM((n, g, SUB, D), F32)
    return pl.pallas_call(
        functools.partial(_mixer_kernel, nsb=nsb, tt=tt, with_q=with_q),
        grid=(ns // nsb, t // tt),
        in_specs=in_specs, out_specs=out_specs, out_shape=out_shape,
        scratch_shapes=[tile(C4_W - 1 + tt), tile(C31_W - 1 + tt), tile(tt), tile(tt), tile(tt),
                        pltpu.VMEM((g, SUB, D), F32),
                        pltpu.VMEM((C4_W, SUB, D), F32), pltpu.VMEM((C31_W, SUB, D), F32)],
        compiler_params=_params(2),
        name="mixer",
    )(*args)


def _memkv_kernel(m_ref, g_ref, wk_ref, wv_ref, k5_ref, v5_ref, kb_ref, vb_ref, *, nb):
    m = _rms(m_ref[...], g_ref[...]).astype(BF)
    k = _dot(m, wk_ref[...])
    v = _dot(m, wv_ref[...])
    k5_ref[...] = k.reshape(nb, N_MEM, N_HEADS, HEAD_D)
    v5_ref[...] = v.reshape(nb, N_MEM, N_HEADS, HEAD_D)
    kb_ref[...] = k.astype(BF)
    vb_ref[...] = v.astype(BF)


def _memkv_call(mem2d, g, wk, wv, nb):
    rows = mem2d.shape[0]
    b = rows // N_MEM
    tm = nb * N_MEM
    kv5 = pl.BlockSpec((nb, N_MEM, N_HEADS, HEAD_D), lambda i: (i, 0, 0, 0))
    row = pl.BlockSpec((tm, D), lambda i: (i, 0))
    return pl.pallas_call(
        functools.partial(_memkv_kernel, nb=nb),
        grid=(b // nb,),
        in_specs=[row, _const_spec((1, D)), _const_spec((D, D)), _const_spec((D, D))],
        out_specs=[kv5, kv5, row, row],
        out_shape=[jax.ShapeDtypeStruct((b, N_MEM, N_HEADS, HEAD_D), F32),
                   jax.ShapeDtypeStruct((b, N_MEM, N_HEADS, HEAD_D), F32),
                   jax.ShapeDtypeStruct((rows, D), BF), jax.ShapeDtypeStruct((rows, D), BF)],
        compiler_params=_params(1),
        name="memkv",
    )(mem2d, g, wk, wv)


def _attn_ffn_prompt_kernel(x_ref, k_ref, v_ref, gq_ref, wq_ref, wo_ref, gao_ref,
                            gpre_ref, wg_ref, wu_ref, wd_ref, gpost_ref, o_ref):
    x = x_ref[...]
    q = _dot(_rms(x, gq_ref[...]).astype(BF), wq_ref[...])
    k = k_ref[...]
    v = v_ref[...]
    outs = []
    for hh in range(N_HEADS):
        sl = slice(hh * HEAD_D, (hh + 1) * HEAD_D)
        s = lax.dot_general(q[:, sl].astype(BF), k[:, sl], (((1,), (1,)), ((), ())),
                            preferred_element_type=F32) * (HEAD_D ** -0.5)
        e = jnp.exp(s - jnp.max(s, axis=-1, keepdims=True))
        p = e / jnp.sum(e, axis=-1, keepdims=True)
        outs.append(_dot(p.astype(BF), v[:, sl]))
    o = jnp.concatenate(outs, axis=-1)
    x3 = x + _rms(_dot(o.astype(BF), wo_ref[...]), gao_ref[...])
    o_ref[...] = _ffn(x3, gpre_ref[...], wg_ref, wu_ref, wd_ref, gpost_ref[...])


def _attn_ffn_prompt_call(x3d, k3d, v3d, gq, wq, wo, gao, gpre, wg, wu, wd, gpost, tm):
    b, s, _ = x3d.shape
    vec = _const_spec((1, D))
    return pl.pallas_call(
        _attn_ffn_prompt_kernel,
        grid=(b, s // tm),
        in_specs=[pl.BlockSpec((None, tm, D), lambda i, j: (i, j, 0)),
                  pl.BlockSpec((None, N_MEM, D), lambda i, j: (i, 0, 0)),
                  pl.BlockSpec((None, N_MEM, D), lambda i, j: (i, 0, 0)),
                  vec, _const_spec((D, D)), _const_spec((D, D)), vec,
                  vec, _const_spec((D, D_FF)), _const_spec((D, D_FF)), _const_spec((D_FF, D)), vec],
        out_specs=pl.BlockSpec((None, tm, D), lambda i, j: (i, j, 0)),
        out_shape=jax.ShapeDtypeStruct((b, s, D), F32),
        compiler_params=_params(2),
        name="attn_ffn_prompt",
    )(x3d, k3d, v3d, gq, wq, wo, gao, gpre, wg, wu, wd, gpost)


def _attn_sample_kernel(q_ref, k_ref, v_ref, o_ref, *, sq):
    rows = N_HEADS * SUB
    cols = N_MEM * N_HEADS
    own_head = (lax.broadcasted_iota(jnp.int32, (rows, cols), 1) % N_HEADS
                == lax.broadcasted_iota(jnp.int32, (rows, cols), 0) // SUB)
    for s in range(sq):
        kn = k_ref[s].reshape(cols, HEAD_D).astype(BF)
        vn = v_ref[s].reshape(cols, HEAD_D).astype(BF)
        q = q_ref[pl.ds(s * SUB, SUB), :]
        qh = jnp.concatenate([q[:, h * HEAD_D:(h + 1) * HEAD_D] for h in range(N_HEADS)], axis=0)
        sc = lax.dot_general(qh.astype(BF), kn, (((1,), (1,)), ((), ())),
                             preferred_element_type=F32) * (HEAD_D ** -0.5)
        sc = jnp.where(own_head, sc, -1e30)
        e = jnp.exp(sc - jnp.max(sc, axis=-1, keepdims=True))
        p = e / jnp.sum(e, axis=-1, keepdims=True)
        oh = _dot(p.astype(BF), vn)
        for h in range(N_HEADS):
            o_ref[pl.ds(s * SUB, SUB), pl.ds(h * HEAD_D, HEAD_D)] = oh[h * SUB:(h + 1) * SUB, :]


def _attn_sample_call(q2d, k5, v5, sq):
    n_seq = k5.shape[1]
    rows = sq * SUB
    kv = pl.BlockSpec((None, sq, N_MEM, N_HEADS, HEAD_D), lambda i: (0, i, 0, 0, 0))
    return pl.pallas_call(
        functools.partial(_attn_sample_kernel, sq=sq),
        grid=(n_seq // sq,),
        in_specs=[pl.BlockSpec((rows, D), lambda i: (i, 0)), kv, kv],
        out_specs=pl.BlockSpec((rows, D), lambda i: (i, 0)),
        out_shape=jax.ShapeDtypeStruct((n_seq * SUB, D), F32),
        compiler_params=_params(1),
        name="attn_sample",
    )(q2d, k5, v5)


def _oproj_ffn_kernel(x_ref, a_ref, wo_ref, gao_ref, gpre_ref, wg_ref, wu_ref, wd_ref, gpost_ref,
                      o_ref):
    x3 = x_ref[...] + _rms(_dot(a_ref[...].astype(BF), wo_ref[...]), gao_ref[...])
    o_ref[...] = _ffn(x3, gpre_ref[...], wg_ref, wu_ref, wd_ref, gpost_ref[...])


def _oproj_ffn_call(x2d, a2d, wo, gao, gpre, wg, wu, wd, gpost, tm):
    rows = x2d.shape[0]
    vec = _const_spec((1, D))
    row = pl.BlockSpec((tm, D), lambda i: (i, 0))
    return pl.pallas_call(
        _oproj_ffn_kernel,
        grid=(rows // tm,),
        in_specs=[row, row, _const_spec((D, D)), vec,
                  vec, _const_spec((D, D_FF)), _const_spec((D, D_FF)), _const_spec((D_FF, D)), vec],
        out_specs=row,
        out_shape=jax.ShapeDtypeStruct((rows, D), F32),
        compiler_params=_params(1),
        name="oproj_ffn",
    )(x2d, a2d, wo, gao, gpre, wg, wu, wd, gpost)


def _pair_gate_weights(w_a, w_x):
    hd = w_a.shape[-1]

    def blockdiag(w):
        w = w.reshape(N_PAIR, 2, hd, hd)
        z = jnp.zeros((N_PAIR, hd, hd), w.dtype)
        top = jnp.concatenate([w[:, 0], z], axis=-1)
        bot = jnp.concatenate([z, w[:, 1]], axis=-1)
        return jnp.concatenate([top, bot], axis=-2)

    return jnp.concatenate([blockdiag(w_a), blockdiag(w_x)], axis=-1).astype(BF)


def _seq_first(a):
    return jnp.swapaxes(a, 0, 1)


def _layer(l, xp, xs, mem_prompt, state_lru_conv, state_lru_h, state_conf_conv, cache_mem_k,
           cache_mem_v, w):
    bp, sp, _ = xp.shape
    bs, ss, _ = xs.shape
    vec = lambda name: w[name][l].reshape(1, D)
    bf = lambda name: w[name][l].astype(BF)

    ff1 = (vec('g_ff1_pre'), bf('ff1_w_gate'), bf('ff1_w_up'), bf('ff1_w_down'), vec('g_ff1_post'))
    ff2 = (vec('g_ff2_pre'), bf('ff2_w_gate'), bf('ff2_w_up'), bf('ff2_w_down'), vec('g_ff2_post'))
    mix = (vec('g_mix_pre'), bf('w_in'), w['lru_conv_w'][l], vec('lru_conv_b'),
           _pair_gate_weights(w['lru_w_a'][l], w['lru_w_x'][l]), vec('lru_b_a'), vec('lru_b_x'),
           vec('lru_lambda'), w['conf_conv_w'][l], vec('conf_conv_b'), vec('conf_ln_g'),
           vec('conf_ln_b'), bf('w_out'), vec('g_mix_post'))
    wq, wo = bf('w_q'), bf('w_o')

    mk5, mv5, mkb, mvb = _memkv_call(mem_prompt.reshape(bp * N_MEM, D), vec('g_mem_kv'),
                                     bf('w_mem_k'), bf('w_mem_v'), nb=2)
    x1 = _ffn_tmajor_call(xp, *ff1, nsb=bp, tt=64)
    x2, c4p, hp, c31p = _mixer_call(
        x1, jnp.zeros((C4_W - 1, bp, D), F32), jnp.zeros((bp, D), F32),
        jnp.zeros((C31_W - 1, bp, D), F32), mix, None, nsb=bp, tt=32)
    xp_out = _attn_ffn_prompt_call(x2, mkb.reshape(bp, N_MEM, D), mvb.reshape(bp, N_MEM, D),
                                   vec('g_mem_pre'), wq, wo, vec('g_mem_post'), *ff2, tm=512)

    s1 = _ffn_tmajor_call(xs, *ff1, nsb=64, tt=ss)
    s2, q, c4s, hs, c31s = _mixer_call(
        s1, _seq_first(state_lru_conv[l]), state_lru_h[l], _seq_first(state_conf_conv[l]), mix,
        (vec('g_mem_pre'), wq), nsb=32, tt=ss)
    att = _attn_sample_call(q.reshape(bs * ss, D), cache_mem_k[l:l + 1], cache_mem_v[l:l + 1], sq=4)
    xs_out = _oproj_ffn_call(s2.reshape(bs * ss, D), att, wo, vec('g_mem_post'), *ff2,
                             tm=512).reshape(bs, ss, D)

    return (xp_out, xs_out, _seq_first(c4p), hp, _seq_first(c31p), mk5, mv5,
            _seq_first(c4s), hs, _seq_first(c31s))


def kernel(x_prompt, x_sample, mem_prompt, state_lru_conv, state_lru_h, state_conf_conv, cache_mem_k, cache_mem_v, g_ff1_pre, ff1_w_gate, ff1_w_up, ff1_w_down, g_ff1_post, g_mix_pre, w_in, lru_conv_w, lru_conv_b, lru_w_a, lru_b_a, lru_w_x, lru_b_x, lru_lambda, conf_conv_w, conf_conv_b, conf_ln_g, conf_ln_b, w_out, g_mix_post, g_mem_pre, g_mem_kv, w_mem_k, w_mem_v, w_q, w_o, g_mem_post, g_ff2_pre, ff2_w_gate, ff2_w_up, ff2_w_down, g_ff2_post):
    w = dict(g_ff1_pre=g_ff1_pre, ff1_w_gate=ff1_w_gate, ff1_w_up=ff1_w_up, ff1_w_down=ff1_w_down,
             g_ff1_post=g_ff1_post, g_mix_pre=g_mix_pre, w_in=w_in, lru_conv_w=lru_conv_w,
             lru_conv_b=lru_conv_b, lru_w_a=lru_w_a, lru_b_a=lru_b_a, lru_w_x=lru_w_x,
             lru_b_x=lru_b_x, lru_lambda=lru_lambda, conf_conv_w=conf_conv_w,
             conf_conv_b=conf_conv_b, conf_ln_g=conf_ln_g, conf_ln_b=conf_ln_b, w_out=w_out,
             g_mix_post=g_mix_post, g_mem_pre=g_mem_pre, g_mem_kv=g_mem_kv, w_mem_k=w_mem_k,
             w_mem_v=w_mem_v, w_q=w_q, w_o=w_o, g_mem_post=g_mem_post, g_ff2_pre=g_ff2_pre,
             ff2_w_gate=ff2_w_gate, ff2_w_up=ff2_w_up, ff2_w_down=ff2_w_down, g_ff2_post=g_ff2_post)
    depth = w_in.shape[0]
    xp, xs = x_prompt, x_sample
    per_layer = []
    for l in range(depth):
        outs = _layer(l, xp, xs, mem_prompt, state_lru_conv, state_lru_h, state_conf_conv,
                      cache_mem_k, cache_mem_v, w)
        xp, xs = outs[0], outs[1]
        per_layer.append(outs[2:])
    stacked = [jnp.stack([pl_[i] for pl_ in per_layer]) for i in range(8)]
    return (xp, xs, *stacked)
```

```python
import functools

import jax
import jax.numpy as jnp
from jax import lax
from jax.experimental import pallas as pl
from jax.experimental.pallas import tpu as pltpu

D = 1024
D_FF = 2816
N_PAIR = 4
PAIR_W = 256
LRU_C = 8.0
C4_W = 4
C31_W = 31
N_MEM = 256
N_HEADS = 4
HEAD_D = 256
FFN_RES = 0.5
EPS = 1e-6
SUB = 8
LANE = 128
CONV_CHUNK = 8
CONV_TAPS = 16
VMEM_LIMIT = 56 * 1024 * 1024

BF = jnp.bfloat16
F32 = jnp.float32


def _dot(a, b):
    return jnp.dot(a, b, preferred_element_type=F32)


def _rms(x, g):
    ms = jnp.mean(x * x, axis=-1, keepdims=True)
    return x * lax.rsqrt(ms + EPS) * g


def _sigmoid(x):
    return 1.0 / (1.0 + jnp.exp(-x))


def _gelu_tanh(x):
    return 0.5 * x * (1.0 + jnp.tanh(0.7978845608028654 * (x + 0.044715 * (x * x * x))))


def _ffn(x, gpre, wg_ref, wu_ref, wd_ref, gpost):
    h = _rms(x, gpre).astype(BF)
    gate = _dot(h, wg_ref[...])
    up = _dot(h, wu_ref[...])
    act = (gate * _sigmoid(gate) * up).astype(BF)
    y = _dot(act, wd_ref[...])
    return x + FFN_RES * _rms(y, gpost)


def _lru_gates(xc, wax_ref, ba, bx, lam):
    xcb = xc.astype(BF)
    r_parts, i_parts = [], []
    for p in range(N_PAIR):
        g2 = _dot(xcb[:, p * PAIR_W:(p + 1) * PAIR_W], wax_ref[p])
        r_parts.append(g2[:, :PAIR_W])
        i_parts.append(g2[:, PAIR_W:])
    r = _sigmoid(jnp.concatenate(r_parts, axis=-1) + ba)
    i = _sigmoid(jnp.concatenate(i_parts, axis=-1) + bx)
    softplus_neg_lam = jnp.maximum(-lam, 0.0) + jnp.log1p(jnp.exp(-jnp.abs(lam)))
    log_a = (-LRU_C * r) * softplus_neg_lam
    a = jnp.exp(log_a)
    th = jnp.tanh(log_a)
    one_minus_a2 = (-2.0 * th) / (1.0 - th)
    u = jnp.sqrt(one_minus_a2) * (i * xc)
    return a, u


def _layernorm(x, g, b):
    xc = x - jnp.mean(x, axis=-1, keepdims=True)
    return xc * lax.rsqrt(jnp.mean(xc * xc, axis=-1, keepdims=True) + EPS) * g + b


def _const_spec(shape):
    n = len(shape)
    return pl.BlockSpec(shape, lambda *_: (0,) * n, pipeline_mode=pl.Buffered(1))


def _params(n_grid):
    return pltpu.CompilerParams(dimension_semantics=("arbitrary",) * n_grid,
                                vmem_limit_bytes=VMEM_LIMIT)


def _ffn_tmajor_kernel(x_ref, gpre_ref, wg_ref, wu_ref, wd_ref, gpost_ref, o_ref, *, nsb, tt):
    y = _ffn(x_ref[...].reshape(nsb * tt, D), gpre_ref[...], wg_ref, wu_ref, wd_ref, gpost_ref[...])
    o_ref[...] = jnp.swapaxes(y.reshape(nsb, tt, D), 0, 1)


def _ffn_tmajor_call(x3d, gpre, wg, wu, wd, gpost, nsb, tt):
    ns, t, _ = x3d.shape
    return pl.pallas_call(
        functools.partial(_ffn_tmajor_kernel, nsb=nsb, tt=tt),
        grid=(ns // nsb, t // tt),
        in_specs=[pl.BlockSpec((nsb, tt, D), lambda i, j: (i, j, 0)),
                  _const_spec((1, D)), _const_spec((D, D_FF)), _const_spec((D, D_FF)),
                  _const_spec((D_FF, D)), _const_spec((1, D))],
        out_specs=pl.BlockSpec((tt, nsb, D), lambda i, j: (j, i, 0)),
        out_shape=jax.ShapeDtypeStruct((t, ns, D), F32),
        compiler_params=_params(2),
        name="ffn1",
    )(x3d, gpre, wg, wu, wd, gpost)


def _mixer_kernel(*refs, nsb, tt, with_q):
    (x_ref, h4_ref, h0_ref, h31_ref, gpre_ref, win_ref, c4w_ref, c4b_ref, wax_ref, ba_ref, bx_ref,
     lam_ref, c31w_ref, c31b_ref, lng_ref, lnb_ref, wout_ref, gpost_ref) = refs[:18]
    refs = refs[18:]
    if with_q:
        gq_ref, wq_ref, o_ref, q_ref = refs[:4]
        refs = refs[4:]
    else:
        o_ref = refs[0]
        refs = refs[1:]
    nc4_ref, nh_ref, nc31_ref, zb, cb, a_buf, u_buf, c_buf, h_carry, w4b, w31b = refs

    g = nsb // SUB
    rows = tt * nsb
    hist4, hist31 = C4_W - 1, C31_W - 1
    n_lc = D // LANE
    j = pl.program_id(1)
    last = j == pl.num_programs(1) - 1

    @pl.when((pl.program_id(0) == 0) & (j == 0))
    def _():
        for k in range(C4_W):
            w4b[k] = jnp.broadcast_to(c4w_ref[pl.ds(k, 1), :], (SUB, D))
        for k in range(C31_W):
            for lc in range(n_lc):
                w31b[lc, k] = jnp.broadcast_to(c31w_ref[pl.ds(k, 1), pl.ds(lc * LANE, LANE)],
                                               (SUB, LANE))
        for lc in range(n_lc):
            w31b[lc, C31_W] = jnp.broadcast_to(c31b_ref[:, pl.ds(lc * LANE, LANE)], (SUB, LANE))

    @pl.when(j == 0)
    def _():
        zb[pl.ds(0, hist4)] = h4_ref[...].reshape(hist4, g, SUB, D)
        for lc in range(n_lc):
            cb[lc, pl.ds(0, hist31)] = h31_ref[:, :, pl.ds(lc * LANE, LANE)].reshape(
                hist31, g, SUB, LANE)
        h_carry[...] = h0_ref[...].reshape(g, SUB, D)

    x = x_ref[...].reshape(rows, D)
    h = _rms(x, gpre_ref[...]).astype(BF)

    zb[pl.ds(hist4, tt)] = _dot(h, win_ref[:, pl.ds(0, D)]).reshape(tt, g, SUB, D)
    xc = jnp.broadcast_to(c4b_ref[...].reshape(1, 1, 1, D), (tt, g, SUB, D))
    for k in range(C4_W):
        xc = xc + w4b[k][None, None] * zb[pl.ds(k, tt)]

    @pl.when(last)
    def _():
        nc4_ref[...] = zb[pl.ds(tt, hist4)].reshape(hist4, nsb, D)

    zb[pl.ds(0, hist4)] = zb[pl.ds(tt, hist4)]

    a, u = _lru_gates(xc.reshape(rows, D), wax_ref, ba_ref[...], bx_ref[...], lam_ref[...])
    a_buf[...] = a.reshape(tt, g, SUB, D)
    u_buf[...] = u.reshape(tt, g, SUB, D)

    def scan_body(t, hc):
        hc = a_buf[t] * hc + u_buf[t]
        u_buf[t] = hc
        return hc

    hc = lax.fori_loop(0, tt, scan_body, h_carry[...])
    h_carry[...] = hc

    @pl.when(last)
    def _():
        nh_ref[...] = hc.reshape(nsb, D)

    z_v = _dot(h, win_ref[:, pl.ds(2 * D, D)])
    z_gate = _dot(h, win_ref[:, pl.ds(3 * D, D)])
    glu = (z_v * _sigmoid(z_gate)).reshape(tt, g, SUB, D)
    for lc in range(n_lc):
        cb[lc, pl.ds(hist31, tt)] = glu[..., lc * LANE:(lc + 1) * LANE]

    rb = min(tt, CONV_CHUNK)

    def conv_body(lc, carry):
        for k0 in range(0, C31_W, CONV_TAPS):
            nk = min(CONV_TAPS, C31_W - k0)
            wk = [w31b[lc, k0 + i] for i in range(nk)]
            for gi in range(g):
                for t0 in range(0, tt, rb):
                    if k0 == 0:
                        acc = [w31b[lc, C31_W] for _ in range(rb)]
                    else:
                        acc = [c_buf[lc, t0 + r, gi] for r in range(rb)]
                    for s in range(rb + nk - 1):
                        xs = cb[lc, t0 + k0 + s, gi]
                        for r in range(rb):
                            if 0 <= s - r < nk:
                                acc[r] = acc[r] + wk[s - r] * xs
                    for r in range(rb):
                        c_buf[lc, t0 + r, gi] = acc[r]
        return carry

    lax.fori_loop(0, n_lc, conv_body, 0)

    @pl.when(last)
    def _():
        for lc in range(n_lc):
            nc31_ref[:, :, pl.ds(lc * LANE, LANE)] = cb[lc, pl.ds(tt, hist31)].reshape(
                hist31, nsb, LANE)

    for lc in range(n_lc):
        for s in range(hist31):
            cb[lc, s] = cb[lc, tt + s]

    z_g = _dot(h, win_ref[:, pl.ds(D, D)])
    y_lru = (_gelu_tanh(z_g) * u_buf[...].reshape(rows, D)).astype(BF)
    c = jnp.concatenate([c_buf[lc].reshape(rows, LANE) for lc in range(n_lc)], axis=-1)
    yc = _layernorm(c, lng_ref[...], lnb_ref[...])
    y_conf = (yc * _sigmoid(yc)).astype(BF)
    out = _dot(y_lru, wout_ref[pl.ds(0, D), :]) + _dot(y_conf, wout_ref[pl.ds(D, D), :])
    x2 = x + _rms(out, gpost_ref[...])
    o_ref[...] = jnp.swapaxes(x2.reshape(tt, nsb, D), 0, 1)
    if with_q:
        q = _dot(_rms(x2, gq_ref[...]).astype(BF), wq_ref[...])
        q_ref[...] = jnp.swapaxes(q.reshape(tt, nsb, D), 0, 1)


def _mixer_call(x_tm, hist4, h0, hist31, mix, q_params, nsb, tt):
    t, ns, _ = x_tm.shape
    with_q = q_params is not None
    vec = _const_spec((1, D))
    seq_major = pl.BlockSpec((nsb, tt, D), lambda i, j: (i, j, 0))
    in_specs = [pl.BlockSpec((tt, nsb, D), lambda i, j: (j, i, 0)),
                pl.BlockSpec((C4_W - 1, nsb, D), lambda i, j: (0, i, 0)),
                pl.BlockSpec((nsb, D), lambda i, j: (i, 0)),
                pl.BlockSpec((C31_W - 1, nsb, D), lambda i, j: (0, i, 0)),
                vec, _const_spec((D, 4 * D)), _const_spec((C4_W, D)), vec,
                _const_spec((N_PAIR, PAIR_W, 2 * PAIR_W)), vec, vec, vec,
                _const_spec((C31_W, D)), vec, vec, vec, _const_spec((2 * D, D)), vec]
    out_specs = [seq_major]
    out_shape = [jax.ShapeDtypeStruct((ns, t, D), F32)]
    args = [x_tm, hist4, h0, hist31, *mix]
    if with_q:
        in_specs += [vec, _const_spec((D, D))]
        out_specs.append(seq_major)
        out_shape.append(jax.ShapeDtypeStruct((ns, t, D), F32))
        args += list(q_params)
    out_specs += [pl.BlockSpec((C4_W - 1, nsb, D), lambda i, j: (0, i, 0)),
                  pl.BlockSpec((nsb, D), lambda i, j: (i, 0)),
                  pl.BlockSpec((C31_W - 1, nsb, D), lambda i, j: (0, i, 0))]
    out_shape += [jax.ShapeDtypeStruct((C4_W - 1, ns, D), F32),
                  jax.ShapeDtypeStruct((ns, D), F32),
                  jax.ShapeDtypeStruct((C31_W - 1, ns, D), F32)]
    g = nsb // SUB
    tile = lambda n: pltpu.VMEM((n, g, SUB, D), F32)
    col = lambda n: pltpu.VMEM((D // LANE, n, g, SUB, LANE), F32)
    return pl.pallas_call(
        functools.partial(_mixer_kernel, nsb=nsb, tt=tt, with_q=with_q),
        grid=(ns // nsb, t // tt),
        in_specs=in_specs, out_specs=out_specs, out_shape=out_shape,
        scratch_shapes=[tile(C4_W - 1 + tt), col(C31_W - 1 + tt), tile(tt), tile(tt), col(tt),
                        pltpu.VMEM((g, SUB, D), F32),
                        pltpu.VMEM((C4_W, SUB, D), F32),
                        pltpu.VMEM((D // LANE, C31_W + 1, SUB, LANE), F32)],
        compiler_params=_params(2),
        name="mixer",
    )(*args)


def _memkv_kernel(m_ref, g_ref, wk_ref, wv_ref, k5_ref, v5_ref, kb_ref, vb_ref, *, nb):
    m = _rms(m_ref[...], g_ref[...]).astype(BF)
    k = _dot(m, wk_ref[...])
    v = _dot(m, wv_ref[...])
    k5_ref[...] = k.reshape(nb, N_MEM, N_HEADS, HEAD_D)
    v5_ref[...] = v.reshape(nb, N_MEM, N_HEADS, HEAD_D)
    kb_ref[...] = k.astype(BF)
    vb_ref[...] = v.astype(BF)


def _memkv_call(mem2d, g, wk, wv, nb):
    rows = mem2d.shape[0]
    b = rows // N_MEM
    tm = nb * N_MEM
    kv5 = pl.BlockSpec((nb, N_MEM, N_HEADS, HEAD_D), lambda i: (i, 0, 0, 0))
    row = pl.BlockSpec((tm, D), lambda i: (i, 0))
    return pl.pallas_call(
        functools.partial(_memkv_kernel, nb=nb),
        grid=(b // nb,),
        in_specs=[row, _const_spec((1, D)), _const_spec((D, D)), _const_spec((D, D))],
        out_specs=[kv5, kv5, row, row],
        out_shape=[jax.ShapeDtypeStruct((b, N_MEM, N_HEADS, HEAD_D), F32),
                   jax.ShapeDtypeStruct((b, N_MEM, N_HEADS, HEAD_D), F32),
                   jax.ShapeDtypeStruct((rows, D), BF), jax.ShapeDtypeStruct((rows, D), BF)],
        compiler_params=_params(1),
        name="memkv",
    )(mem2d, g, wk, wv)


def _attn_ffn_prompt_kernel(x_ref, k_ref, v_ref, gq_ref, wq_ref, wo_ref, gao_ref,
                            gpre_ref, wg_ref, wu_ref, wd_ref, gpost_ref, o_ref):
    x = x_ref[...]
    q = _dot(_rms(x, gq_ref[...]).astype(BF), wq_ref[...])
    k = k_ref[...]
    v = v_ref[...]
    outs = []
    for hh in range(N_HEADS):
        sl = slice(hh * HEAD_D, (hh + 1) * HEAD_D)
        s = lax.dot_general(q[:, sl].astype(BF), k[:, sl], (((1,), (1,)), ((), ())),
                            preferred_element_type=F32) * (HEAD_D ** -0.5)
        e = jnp.exp(s - jnp.max(s, axis=-1, keepdims=True))
        p = e / jnp.sum(e, axis=-1, keepdims=True)
        outs.append(_dot(p.astype(BF), v[:, sl]))
    o = jnp.concatenate(outs, axis=-1)
    x3 = x + _rms(_dot(o.astype(BF), wo_ref[...]), gao_ref[...])
    o_ref[...] = _ffn(x3, gpre_ref[...], wg_ref, wu_ref, wd_ref, gpost_ref[...])


def _attn_ffn_prompt_call(x3d, k3d, v3d, gq, wq, wo, gao, gpre, wg, wu, wd, gpost, tm):
    b, s, _ = x3d.shape
    vec = _const_spec((1, D))
    return pl.pallas_call(
        _attn_ffn_prompt_kernel,
        grid=(b, s // tm),
        in_specs=[pl.BlockSpec((None, tm, D), lambda i, j: (i, j, 0)),
                  pl.BlockSpec((None, N_MEM, D), lambda i, j: (i, 0, 0)),
                  pl.BlockSpec((None, N_MEM, D), lambda i, j: (i, 0, 0)),
                  vec, _const_spec((D, D)), _const_spec((D, D)), vec,
                  vec, _const_spec((D, D_FF)), _const_spec((D, D_FF)), _const_spec((D_FF, D)), vec],
        out_specs=pl.BlockSpec((None, tm, D), lambda i, j: (i, j, 0)),
        out_shape=jax.ShapeDtypeStruct((b, s, D), F32),
        compiler_params=_params(2),
        name="attn_ffn_prompt",
    )(x3d, k3d, v3d, gq, wq, wo, gao, gpre, wg, wu, wd, gpost)


def _attn_sample_kernel(q_ref, k_ref, v_ref, o_ref, *, sq):
    rows = N_HEADS * SUB
    cols = N_MEM * N_HEADS
    own_head = (lax.broadcasted_iota(jnp.int32, (rows, cols), 1) % N_HEADS
                == lax.broadcasted_iota(jnp.int32, (rows, cols), 0) // SUB)
    for s in range(sq):
        kn = k_ref[s].reshape(cols, HEAD_D).astype(BF)
        vn = v_ref[s].reshape(cols, HEAD_D).astype(BF)
        q = q_ref[pl.ds(s * SUB, SUB), :]
        qh = jnp.concatenate([q[:, h * HEAD_D:(h + 1) * HEAD_D] for h in range(N_HEADS)], axis=0)
        sc = lax.dot_general(qh.astype(BF), kn, (((1,), (1,)), ((), ())),
                             preferred_element_type=F32) * (HEAD_D ** -0.5)
        sc = jnp.where(own_head, sc, -1e30)
        e = jnp.exp(sc - jnp.max(sc, axis=-1, keepdims=True))
        p = e / jnp.sum(e, axis=-1, keepdims=True)
        oh = _dot(p.astype(BF), vn)
        for h in range(N_HEADS):
            o_ref[pl.ds(s * SUB, SUB), pl.ds(h * HEAD_D, HEAD_D)] = oh[h * SUB:(h + 1) * SUB, :]


def _attn_sample_call(q2d, k5, v5, sq):
    n_seq = k5.shape[1]
    rows = sq * SUB
    kv = pl.BlockSpec((None, sq, N_MEM, N_HEADS, HEAD_D), lambda i: (0, i, 0, 0, 0))
    return pl.pallas_call(
        functools.partial(_attn_sample_kernel, sq=sq),
        grid=(n_seq // sq,),
        in_specs=[pl.BlockSpec((rows, D), lambda i: (i, 0)), kv, kv],
        out_specs=pl.BlockSpec((rows, D), lambda i: (i, 0)),
        out_shape=jax.ShapeDtypeStruct((n_seq * SUB, D), F32),
        compiler_params=_params(1),
        name="attn_sample",
    )(q2d, k5, v5)


def _oproj_ffn_kernel(x_ref, a_ref, wo_ref, gao_ref, gpre_ref, wg_ref, wu_ref, wd_ref, gpost_ref,
                      o_ref):
    x3 = x_ref[...] + _rms(_dot(a_ref[...].astype(BF), wo_ref[...]), gao_ref[...])
    o_ref[...] = _ffn(x3, gpre_ref[...], wg_ref, wu_ref, wd_ref, gpost_ref[...])


def _oproj_ffn_call(x2d, a2d, wo, gao, gpre, wg, wu, wd, gpost, tm):
    rows = x2d.shape[0]
    vec = _const_spec((1, D))
    row = pl.BlockSpec((tm, D), lambda i: (i, 0))
    return pl.pallas_call(
        _oproj_ffn_kernel,
        grid=(rows // tm,),
        in_specs=[row, row, _const_spec((D, D)), vec,
                  vec, _const_spec((D, D_FF)), _const_spec((D, D_FF)), _const_spec((D_FF, D)), vec],
        out_specs=row,
        out_shape=jax.ShapeDtypeStruct((rows, D), F32),
        compiler_params=_params(1),
        name="oproj_ffn",
    )(x2d, a2d, wo, gao, gpre, wg, wu, wd, gpost)


def _pair_gate_weights(w_a, w_x):
    hd = w_a.shape[-1]

    def blockdiag(w):
        w = w.reshape(N_PAIR, 2, hd, hd)
        z = jnp.zeros((N_PAIR, hd, hd), w.dtype)
        top = jnp.concatenate([w[:, 0], z], axis=-1)
        bot = jnp.concatenate([z, w[:, 1]], axis=-1)
        return jnp.concatenate([top, bot], axis=-2)

    return jnp.concatenate([blockdiag(w_a), blockdiag(w_x)], axis=-1).astype(BF)


def _seq_first(a):
    return jnp.swapaxes(a, 0, 1)


def _layer(l, xp, xs, mem_prompt, state_lru_conv, state_lru_h, state_conf_conv, cache_mem_k,
           cache_mem_v, w):
    bp, sp, _ = xp.shape
    bs, ss, _ = xs.shape
    vec = lambda name: w[name][l].reshape(1, D)
    bf = lambda name: w[name][l].astype(BF)

    ff1 = (vec('g_ff1_pre'), bf('ff1_w_gate'), bf('ff1_w_up'), bf('ff1_w_down'), vec('g_ff1_post'))
    ff2 = (vec('g_ff2_pre'), bf('ff2_w_gate'), bf('ff2_w_up'), bf('ff2_w_down'), vec('g_ff2_post'))
    mix = (vec('g_mix_pre'), bf('w_in'), w['lru_conv_w'][l], vec('lru_conv_b'),
           _pair_gate_weights(w['lru_w_a'][l], w['lru_w_x'][l]), vec('lru_b_a'), vec('lru_b_x'),
           vec('lru_lambda'), w['conf_conv_w'][l], vec('conf_conv_b'), vec('conf_ln_g'),
           vec('conf_ln_b'), bf('w_out'), vec('g_mix_post'))
    wq, wo = bf('w_q'), bf('w_o')

    mk5, mv5, mkb, mvb = _memkv_call(mem_prompt.reshape(bp * N_MEM, D), vec('g_mem_kv'),
                                     bf('w_mem_k'), bf('w_mem_v'), nb=2)
    x1 = _ffn_tmajor_call(xp, *ff1, nsb=bp, tt=64)
    x2, c4p, hp, c31p = _mixer_call(
        x1, jnp.zeros((C4_W - 1, bp, D), F32), jnp.zeros((bp, D), F32),
        jnp.zeros((C31_W - 1, bp, D), F32), mix, None, nsb=bp, tt=32)
    xp_out = _attn_ffn_prompt_call(x2, mkb.reshape(bp, N_MEM, D), mvb.reshape(bp, N_MEM, D),
                                   vec('g_mem_pre'), wq, wo, vec('g_mem_post'), *ff2, tm=512)

    s1 = _ffn_tmajor_call(xs, *ff1, nsb=64, tt=ss)
    s2, q, c4s, hs, c31s = _mixer_call(
        s1, _seq_first(state_lru_conv[l]), state_lru_h[l], _seq_first(state_conf_conv[l]), mix,
        (vec('g_mem_pre'), wq), nsb=32, tt=ss)
    att = _attn_sample_call(q.reshape(bs * ss, D), cache_mem_k[l:l + 1], cache_mem_v[l:l + 1], sq=4)
    xs_out = _oproj_ffn_call(s2.reshape(bs * ss, D), att, wo, vec('g_mem_post'), *ff2,
                             tm=512).reshape(bs, ss, D)

    return (xp_out, xs_out, _seq_first(c4p), hp, _seq_first(c31p), mk5, mv5,
            _seq_first(c4s), hs, _seq_first(c31s))


def kernel(x_prompt, x_sample, mem_prompt, state_lru_conv, state_lru_h, state_conf_conv, cache_mem_k, cache_mem_v, g_ff1_pre, ff1_w_gate, ff1_w_up, ff1_w_down, g_ff1_post, g_mix_pre, w_in, lru_conv_w, lru_conv_b, lru_w_a, lru_b_a, lru_w_x, lru_b_x, lru_lambda, conf_conv_w, conf_conv_b, conf_ln_g, conf_ln_b, w_out, g_mix_post, g_mem_pre, g_mem_kv, w_mem_k, w_mem_v, w_q, w_o, g_mem_post, g_ff2_pre, ff2_w_gate, ff2_w_up, ff2_w_down, g_ff2_post):
    w = dict(g_ff1_pre=g_ff1_pre, ff1_w_gate=ff1_w_gate, ff1_w_up=ff1_w_up, ff1_w_down=ff1_w_down,
             g_ff1_post=g_ff1_post, g_mix_pre=g_mix_pre, w_in=w_in, lru_conv_w=lru_conv_w,
             lru_conv_b=lru_conv_b, lru_w_a=lru_w_a, lru_b_a=lru_b_a, lru_w_x=lru_w_x,
             lru_b_x=lru_b_x, lru_lambda=lru_lambda, conf_conv_w=conf_conv_w,
             conf_conv_b=conf_conv_b, conf_ln_g=conf_ln_g, conf_ln_b=conf_ln_b, w_out=w_out,
             g_mix_post=g_mix_post, g_mem_pre=g_mem_pre, g_mem_kv=g_mem_kv, w_mem_k=w_mem_k,
             w_mem_v=w_mem_v, w_q=w_q, w_o=w_o, g_mem_post=g_mem_post, g_ff2_pre=g_ff2_pre,
             ff2_w_gate=ff2_w_gate, ff2_w_up=ff2_w_up, ff2_w_down=ff2_w_down, g_ff2_post=g_ff2_post)
    depth = w_in.shape[0]
    xp, xs = x_prompt, x_sample
    per_layer = []
    for l in range(depth):
        outs = _layer(l, xp, xs, mem_prompt, state_lru_conv, state_lru_h, state_conf_conv,
                      cache_mem_k, cache_mem_v, w)
        xp, xs = outs[0], outs[1]
        per_layer.append(outs[2:])
    stacked = [jnp.stack([pl_[i] for pl_ in per_layer]) for i in range(8)]
    return (xp, xs, *stacked)
```

```python
import functools

import jax
import jax.numpy as jnp
from jax import lax
from jax.experimental import pallas as pl
from jax.experimental.pallas import tpu as pltpu

D = 1024
D_FF = 2816
N_PAIR = 4
PAIR_W = 256
LRU_C = 8.0
C4_W = 4
C31_W = 31
N_MEM = 256
N_HEADS = 4
HEAD_D = 256
FFN_RES = 0.5
EPS = 1e-6
SUB = 8
LANE = 128
CONV_CHUNK = 8
CONV_TAPS = 16
VMEM_LIMIT = 56 * 1024 * 1024

BF = jnp.bfloat16
F32 = jnp.float32


def _dot(a, b):
    return jnp.dot(a, b, preferred_element_type=F32)


def _rms(x, g):
    ms = jnp.mean(x * x, axis=-1, keepdims=True)
    return x * lax.rsqrt(ms + EPS) * g


def _sigmoid(x):
    return 1.0 / (1.0 + jnp.exp(-x))


def _gelu_tanh(x):
    return 0.5 * x * (1.0 + jnp.tanh(0.7978845608028654 * (x + 0.044715 * (x * x * x))))


def _ffn(x, gpre, wg_ref, wu_ref, wd_ref, gpost):
    h = _rms(x, gpre).astype(BF)
    gate = _dot(h, wg_ref[...])
    up = _dot(h, wu_ref[...])
    act = (gate * _sigmoid(gate) * up).astype(BF)
    y = _dot(act, wd_ref[...])
    return x + FFN_RES * _rms(y, gpost)


def _gate_math(r_pre, i_pre, lam, xc):
    r = _sigmoid(r_pre)
    i = _sigmoid(i_pre)
    softplus_neg_lam = jnp.maximum(-lam, 0.0) + jnp.log1p(jnp.exp(-jnp.abs(lam)))
    log_a = (-LRU_C * r) * softplus_neg_lam
    a = jnp.exp(log_a)
    th = jnp.tanh(log_a)
    y = (-2.0 * th) / (1.0 - th)
    root = jnp.where(y == 0.0, 0.0, y * lax.rsqrt(y))
    return a, root * (i * xc)


def _layernorm(x, g, b):
    xc = x - jnp.mean(x, axis=-1, keepdims=True)
    return xc * lax.rsqrt(jnp.mean(xc * xc, axis=-1, keepdims=True) + EPS) * g + b


def _const_spec(shape):
    n = len(shape)
    return pl.BlockSpec(shape, lambda *_: (0,) * n, pipeline_mode=pl.Buffered(1))


def _params(n_grid):
    return pltpu.CompilerParams(dimension_semantics=("arbitrary",) * n_grid,
                                vmem_limit_bytes=VMEM_LIMIT)


def _ffn_tmajor_kernel(x_ref, gpre_ref, wg_ref, wu_ref, wd_ref, gpost_ref, o_ref, *, nsb, tt):
    hs = nsb // 2
    for i in range(2):
        x = x_ref[pl.ds(i * hs, hs)].reshape(hs * tt, D)
        y = _ffn(x, gpre_ref[...], wg_ref, wu_ref, wd_ref, gpost_ref[...])
        o_ref[:, pl.ds(i * hs, hs), :] = jnp.swapaxes(y.reshape(hs, tt, D), 0, 1)


def _ffn_tmajor_call(x3d, gpre, wg, wu, wd, gpost, nsb, tt):
    ns, t, _ = x3d.shape
    return pl.pallas_call(
        functools.partial(_ffn_tmajor_kernel, nsb=nsb, tt=tt),
        grid=(ns // nsb, t // tt),
        in_specs=[pl.BlockSpec((nsb, tt, D), lambda i, j: (i, j, 0)),
                  _const_spec((1, D)), _const_spec((D, D_FF)), _const_spec((D, D_FF)),
                  _const_spec((D_FF, D)), _const_spec((1, D))],
        out_specs=pl.BlockSpec((tt, nsb, D), lambda i, j: (j, i, 0)),
        out_shape=jax.ShapeDtypeStruct((t, ns, D), F32),
        compiler_params=_params(2),
        name="ffn1",
    )(x3d, gpre, wg, wu, wd, gpost)


def _mixer_kernel(*refs, nsb, tt, with_q):
    (x_ref, h4_ref, h0_ref, h31_ref, gpre_ref, win_ref, c4w_ref, c4b_ref, wax_ref, ba_ref, bx_ref,
     lam_ref, c31w_ref, c31b_ref, lng_ref, lnb_ref, wout_ref, gpost_ref) = refs[:18]
    refs = refs[18:]
    if with_q:
        gq_ref, wq_ref, o_ref, q_ref = refs[:4]
        refs = refs[4:]
    else:
        o_ref = refs[0]
        refs = refs[1:]
    nc4_ref, nh_ref, nc31_ref, zb, cb, a_buf, u_buf, c_buf, gelu_buf, h_carry, w4b, w31b = refs

    g = nsb // SUB
    rows = tt * nsb
    hist4, hist31 = C4_W - 1, C31_W - 1
    n_lc = D // LANE
    j = pl.program_id(1)
    last = j == pl.num_programs(1) - 1

    @pl.when((pl.program_id(0) == 0) & (j == 0))
    def _():
        for k in range(C4_W):
            w4b[k] = jnp.broadcast_to(c4w_ref[pl.ds(k, 1), :], (SUB, D))
        for k in range(C31_W):
            for lc in range(n_lc):
                w31b[lc, k] = jnp.broadcast_to(c31w_ref[pl.ds(k, 1), pl.ds(lc * LANE, LANE)],
                                               (SUB, LANE))
        for lc in range(n_lc):
            w31b[lc, C31_W] = jnp.broadcast_to(c31b_ref[:, pl.ds(lc * LANE, LANE)], (SUB, LANE))

    @pl.when(j == 0)
    def _():
        zb[pl.ds(0, hist4)] = h4_ref[...].reshape(hist4, g, SUB, D)
        for lc in range(n_lc):
            cb[lc, pl.ds(0, hist31)] = h31_ref[:, :, pl.ds(lc * LANE, LANE)].reshape(
                hist31, g, SUB, LANE)
        h_carry[...] = h0_ref[...].reshape(g, SUB, D)

    x = x_ref[...].reshape(rows, D)
    h = _rms(x, gpre_ref[...]).astype(BF)

    zb[pl.ds(hist4, tt)] = _dot(h, win_ref[:, pl.ds(0, D)]).reshape(tt, g, SUB, D)
    xc = jnp.broadcast_to(c4b_ref[...].reshape(1, 1, 1, D), (tt, g, SUB, D))
    for k in range(C4_W):
        xc = xc + w4b[k][None, None] * zb[pl.ds(k, tt)]

    @pl.when(last)
    def _():
        nc4_ref[...] = zb[pl.ds(tt, hist4)].reshape(hist4, nsb, D)

    zb[pl.ds(0, hist4)] = zb[pl.ds(tt, hist4)]

    xcb = xc.reshape(rows, D).astype(BF)
    xc2 = xc.reshape(rows, D)
    for p in range(N_PAIR):
        sl = slice(p * PAIR_W, (p + 1) * PAIR_W)
        z_v = _dot(h, win_ref[:, pl.ds(2 * D + p * PAIR_W, PAIR_W)])
        z_gate = _dot(h, win_ref[:, pl.ds(3 * D + p * PAIR_W, PAIR_W)])
        g2 = _dot(xcb[:, sl], wax_ref[p])
        a, u = _gate_math(g2[:, :PAIR_W] + ba_ref[:, sl], g2[:, PAIR_W:] + bx_ref[:, sl],
                          lam_ref[:, sl], xc2[:, sl])
        a_buf[:, :, :, sl] = a.reshape(tt, g, SUB, PAIR_W)
        u_buf[:, :, :, sl] = u.reshape(tt, g, SUB, PAIR_W)
        gelu_buf[:, sl] = _gelu_tanh(_dot(h, win_ref[:, pl.ds(D + p * PAIR_W, PAIR_W)]))
        glu = (z_v * _sigmoid(z_gate)).reshape(tt, g, SUB, PAIR_W)
        for i_ in range(PAIR_W // LANE):
            cb[p * (PAIR_W // LANE) + i_, pl.ds(hist31, tt)] = glu[..., i_ * LANE:(i_ + 1) * LANE]

    def scan_body(t, hc):
        hc = a_buf[t] * hc + u_buf[t]
        u_buf[t] = hc
        return hc

    hc = lax.fori_loop(0, tt, scan_body, h_carry[...])
    h_carry[...] = hc

    @pl.when(last)
    def _():
        nh_ref[...] = hc.reshape(nsb, D)

    rb = min(tt, CONV_CHUNK)

    def conv_body(lc, carry):
        for k0 in range(0, C31_W, CONV_TAPS):
            nk = min(CONV_TAPS, C31_W - k0)
            wk = [w31b[lc, k0 + i] for i in range(nk)]
            for gi in range(g):
                for t0 in range(0, tt, rb):
                    if k0 == 0:
                        acc = [w31b[lc, C31_W] for _ in range(rb)]
                    else:
                        acc = [c_buf[lc, t0 + r, gi] for r in range(rb)]
                    for s in range(rb + nk - 1):
                        xs = cb[lc, t0 + k0 + s, gi]
                        for r in range(rb):
                            if 0 <= s - r < nk:
                                acc[r] = acc[r] + wk[s - r] * xs
                    for r in range(rb):
                        c_buf[lc, t0 + r, gi] = acc[r]
        return carry

    lax.fori_loop(0, n_lc, conv_body, 0)

    @pl.when(last)
    def _():
        for lc in range(n_lc):
            nc31_ref[:, :, pl.ds(lc * LANE, LANE)] = cb[lc, pl.ds(tt, hist31)].reshape(
                hist31, nsb, LANE)

    for lc in range(n_lc):
        for s in range(hist31):
            cb[lc, s] = cb[lc, tt + s]

    y_lru = (gelu_buf[...] * u_buf[...].reshape(rows, D)).astype(BF)
    out_lru = _dot(y_lru, wout_ref[pl.ds(0, D), :])
    c = jnp.concatenate([c_buf[lc].reshape(rows, LANE) for lc in range(n_lc)], axis=-1)
    yc = _layernorm(c, lng_ref[...], lnb_ref[...])
    y_conf = (yc * _sigmoid(yc)).astype(BF)
    th_ = tt // 2
    for hf in range(2):
        rs = slice(hf * th_ * nsb, (hf + 1) * th_ * nsb)
        out = out_lru[rs] + _dot(y_conf[rs], wout_ref[pl.ds(D, D), :])
        x2 = x[rs] + _rms(out, gpost_ref[...])
        o_ref[:, pl.ds(hf * th_, th_), :] = jnp.swapaxes(x2.reshape(th_, nsb, D), 0, 1)
        if with_q:
            q = _dot(_rms(x2, gq_ref[...]).astype(BF), wq_ref[...])
            q_ref[:, pl.ds(hf * th_, th_), :] = jnp.swapaxes(q.reshape(th_, nsb, D), 0, 1)


def _mixer_call(x_tm, hist4, h0, hist31, mix, q_params, nsb, tt):
    t, ns, _ = x_tm.shape
    with_q = q_params is not None
    vec = _const_spec((1, D))
    seq_major = pl.BlockSpec((nsb, tt, D), lambda i, j: (i, j, 0))
    in_specs = [pl.BlockSpec((tt, nsb, D), lambda i, j: (j, i, 0)),
                pl.BlockSpec((C4_W - 1, nsb, D), lambda i, j: (0, i, 0)),
                pl.BlockSpec((nsb, D), lambda i, j: (i, 0)),
                pl.BlockSpec((C31_W - 1, nsb, D), lambda i, j: (0, i, 0)),
                vec, _const_spec((D, 4 * D)), _const_spec((C4_W, D)), vec,
                _const_spec((N_PAIR, PAIR_W, 2 * PAIR_W)), vec, vec, vec,
                _const_spec((C31_W, D)), vec, vec, vec, _const_spec((2 * D, D)), vec]
    out_specs = [seq_major]
    out_shape = [jax.ShapeDtypeStruct((ns, t, D), F32)]
    args = [x_tm, hist4, h0, hist31, *mix]
    if with_q:
        in_specs += [vec, _const_spec((D, D))]
        out_specs.append(seq_major)
        out_shape.append(jax.ShapeDtypeStruct((ns, t, D), F32))
        args += list(q_params)
    out_specs += [pl.BlockSpec((C4_W - 1, nsb, D), lambda i, j: (0, i, 0)),
                  pl.BlockSpec((nsb, D), lambda i, j: (i, 0)),
                  pl.BlockSpec((C31_W - 1, nsb, D), lambda i, j: (0, i, 0))]
    out_shape += [jax.ShapeDtypeStruct((C4_W - 1, ns, D), F32),
                  jax.ShapeDtypeStruct((ns, D), F32),
                  jax.ShapeDtypeStruct((C31_W - 1, ns, D), F32)]
    g = nsb // SUB
    tile = lambda n: pltpu.VMEM((n, g, SUB, D), F32)
    col = lambda n: pltpu.VMEM((D // LANE, n, g, SUB, LANE), F32)
    return pl.pallas_call(
        functools.partial(_mixer_kernel, nsb=nsb, tt=tt, with_q=with_q),
        grid=(ns // nsb, t // tt),
        in_specs=in_specs, out_specs=out_specs, out_shape=out_shape,
        scratch_shapes=[tile(C4_W - 1 + tt), col(C31_W - 1 + tt), tile(tt), tile(tt), col(tt),
                        pltpu.VMEM((tt * nsb, D), F32),
                        pltpu.VMEM((g, SUB, D), F32),
                        pltpu.VMEM((C4_W, SUB, D), F32),
                        pltpu.VMEM((D // LANE, C31_W + 1, SUB, LANE), F32)],
        compiler_params=_params(2),
        name="mixer",
    )(*args)


def _memkv_kernel(m_ref, g_ref, wk_ref, wv_ref, k5_ref, v5_ref, kb_ref, vb_ref, *, nb):
    m = _rms(m_ref[...], g_ref[...]).astype(BF)
    k = _dot(m, wk_ref[...])
    v = _dot(m, wv_ref[...])
    k5_ref[...] = k.reshape(nb, N_MEM, N_HEADS, HEAD_D)
    v5_ref[...] = v.reshape(nb, N_MEM, N_HEADS, HEAD_D)
    kb_ref[...] = k.astype(BF)
    vb_ref[...] = v.astype(BF)


def _memkv_call(mem2d, g, wk, wv, nb):
    rows = mem2d.shape[0]
    b = rows // N_MEM
    tm = nb * N_MEM
    kv5 = pl.BlockSpec((nb, N_MEM, N_HEADS, HEAD_D), lambda i: (i, 0, 0, 0))
    row = pl.BlockSpec((tm, D), lambda i: (i, 0))
    return pl.pallas_call(
        functools.partial(_memkv_kernel, nb=nb),
        grid=(b // nb,),
        in_specs=[row, _const_spec((1, D)), _const_spec((D, D)), _const_spec((D, D))],
        out_specs=[kv5, kv5, row, row],
        out_shape=[jax.ShapeDtypeStruct((b, N_MEM, N_HEADS, HEAD_D), F32),
                   jax.ShapeDtypeStruct((b, N_MEM, N_HEADS, HEAD_D), F32),
                   jax.ShapeDtypeStruct((rows, D), BF), jax.ShapeDtypeStruct((rows, D), BF)],
        compiler_params=_params(1),
        name="memkv",
    )(mem2d, g, wk, wv)


def _attn_ffn_prompt_kernel(x_ref, k_ref, v_ref, gq_ref, wq_ref, wo_ref, gao_ref,
                            gpre_ref, wg_ref, wu_ref, wd_ref, gpost_ref, o_ref):
    x = x_ref[...]
    q = _dot(_rms(x, gq_ref[...]).astype(BF), wq_ref[...])
    k = k_ref[...]
    v = v_ref[...]
    outs = []
    for hh in range(N_HEADS):
        sl = slice(hh * HEAD_D, (hh + 1) * HEAD_D)
        s = lax.dot_general(q[:, sl].astype(BF), k[:, sl], (((1,), (1,)), ((), ())),
                            preferred_element_type=F32) * (HEAD_D ** -0.5)
        e = jnp.exp(s - jnp.max(s, axis=-1, keepdims=True))
        p = e / jnp.sum(e, axis=-1, keepdims=True)
        outs.append(_dot(p.astype(BF), v[:, sl]))
    o = jnp.concatenate(outs, axis=-1)
    x3 = x + _rms(_dot(o.astype(BF), wo_ref[...]), gao_ref[...])
    o_ref[...] = _ffn(x3, gpre_ref[...], wg_ref, wu_ref, wd_ref, gpost_ref[...])


def _attn_ffn_prompt_call(x3d, k3d, v3d, gq, wq, wo, gao, gpre, wg, wu, wd, gpost, tm):
    b, s, _ = x3d.shape
    vec = _const_spec((1, D))
    return pl.pallas_call(
        _attn_ffn_prompt_kernel,
        grid=(b, s // tm),
        in_specs=[pl.BlockSpec((None, tm, D), lambda i, j: (i, j, 0)),
                  pl.BlockSpec((None, N_MEM, D), lambda i, j: (i, 0, 0)),
                  pl.BlockSpec((None, N_MEM, D), lambda i, j: (i, 0, 0)),
                  vec, _const_spec((D, D)), _const_spec((D, D)), vec,
                  vec, _const_spec((D, D_FF)), _const_spec((D, D_FF)), _const_spec((D_FF, D)), vec],
        out_specs=pl.BlockSpec((None, tm, D), lambda i, j: (i, j, 0)),
        out_shape=jax.ShapeDtypeStruct((b, s, D), F32),
        compiler_params=_params(2),
        name="attn_ffn_prompt",
    )(x3d, k3d, v3d, gq, wq, wo, gao, gpre, wg, wu, wd, gpost)


def _attn_sample_kernel(q_ref, k_ref, v_ref, o_ref, *, sq):
    rows = N_HEADS * SUB
    cols = N_MEM * N_HEADS
    own_head = (lax.broadcasted_iota(jnp.int32, (rows, cols), 1) % N_HEADS
                == lax.broadcasted_iota(jnp.int32, (rows, cols), 0) // SUB)
    for s in range(sq):
        kn = k_ref[s].reshape(cols, HEAD_D).astype(BF)
        vn = v_ref[s].reshape(cols, HEAD_D).astype(BF)
        q = q_ref[pl.ds(s * SUB, SUB), :]
        qh = jnp.concatenate([q[:, h * HEAD_D:(h + 1) * HEAD_D] for h in range(N_HEADS)], axis=0)
        sc = lax.dot_general(qh.astype(BF), kn, (((1,), (1,)), ((), ())),
                             preferred_element_type=F32) * (HEAD_D ** -0.5)
        sc = jnp.where(own_head, sc, -1e30)
        e = jnp.exp(sc - jnp.max(sc, axis=-1, keepdims=True))
        p = e / jnp.sum(e, axis=-1, keepdims=True)
        oh = _dot(p.astype(BF), vn)
        for h in range(N_HEADS):
            o_ref[pl.ds(s * SUB, SUB), pl.ds(h * HEAD_D, HEAD_D)] = oh[h * SUB:(h + 1) * SUB, :]


def _attn_sample_call(q2d, k5, v5, sq):
    n_seq = k5.shape[1]
    rows = sq * SUB
    kv = pl.BlockSpec((None, sq, N_MEM, N_HEADS, HEAD_D), lambda i: (0, i, 0, 0, 0))
    return pl.pallas_call(
        functools.partial(_attn_sample_kernel, sq=sq),
        grid=(n_seq // sq,),
        in_specs=[pl.BlockSpec((rows, D), lambda i: (i, 0)), kv, kv],
        out_specs=pl.BlockSpec((rows, D), lambda i: (i, 0)),
        out_shape=jax.ShapeDtypeStruct((n_seq * SUB, D), F32),
        compiler_params=_params(1),
        name="attn_sample",
    )(q2d, k5, v5)


def _oproj_ffn_kernel(x_ref, a_ref, wo_ref, gao_ref, gpre_ref, wg_ref, wu_ref, wd_ref, gpost_ref,
                      o_ref):
    x3 = x_ref[...] + _rms(_dot(a_ref[...].astype(BF), wo_ref[...]), gao_ref[...])
    o_ref[...] = _ffn(x3, gpre_ref[...], wg_ref, wu_ref, wd_ref, gpost_ref[...])


def _oproj_ffn_call(x2d, a2d, wo, gao, gpre, wg, wu, wd, gpost, tm):
    rows = x2d.shape[0]
    vec = _const_spec((1, D))
    row = pl.BlockSpec((tm, D), lambda i: (i, 0))
    return pl.pallas_call(
        _oproj_ffn_kernel,
        grid=(rows // tm,),
        in_specs=[row, row, _const_spec((D, D)), vec,
                  vec, _const_spec((D, D_FF)), _const_spec((D, D_FF)), _const_spec((D_FF, D)), vec],
        out_specs=row,
        out_shape=jax.ShapeDtypeStruct((rows, D), F32),
        compiler_params=_params(1),
        name="oproj_ffn",
    )(x2d, a2d, wo, gao, gpre, wg, wu, wd, gpost)


def _pair_gate_weights(w_a, w_x):
    hd = w_a.shape[-1]

    def blockdiag(w):
        w = w.reshape(N_PAIR, 2, hd, hd)
        z = jnp.zeros((N_PAIR, hd, hd), w.dtype)
        top = jnp.concatenate([w[:, 0], z], axis=-1)
        bot = jnp.concatenate([z, w[:, 1]], axis=-1)
        return jnp.concatenate([top, bot], axis=-2)

    return jnp.concatenate([blockdiag(w_a), blockdiag(w_x)], axis=-1).astype(BF)


def _seq_first(a):
    return jnp.swapaxes(a, 0, 1)


def _layer(l, xp, xs, mem_prompt, state_lru_conv, state_lru_h, state_conf_conv, cache_mem_k,
           cache_mem_v, w):
    bp, sp, _ = xp.shape
    bs, ss, _ = xs.shape
    vec = lambda name: w[name][l].reshape(1, D)
    bf = lambda name: w[name][l].astype(BF)

    ff1 = (vec('g_ff1_pre'), bf('ff1_w_gate'), bf('ff1_w_up'), bf('ff1_w_down'), vec('g_ff1_post'))
    ff2 = (vec('g_ff2_pre'), bf('ff2_w_gate'), bf('ff2_w_up'), bf('ff2_w_down'), vec('g_ff2_post'))
    mix = (vec('g_mix_pre'), bf('w_in'), w['lru_conv_w'][l], vec('lru_conv_b'),
           _pair_gate_weights(w['lru_w_a'][l], w['lru_w_x'][l]), vec('lru_b_a'), vec('lru_b_x'),
           vec('lru_lambda'), w['conf_conv_w'][l], vec('conf_conv_b'), vec('conf_ln_g'),
           vec('conf_ln_b'), bf('w_out'), vec('g_mix_post'))
    wq, wo = bf('w_q'), bf('w_o')

    mk5, mv5, mkb, mvb = _memkv_call(mem_prompt.reshape(bp * N_MEM, D), vec('g_mem_kv'),
                                     bf('w_mem_k'), bf('w_mem_v'), nb=2)
    x1 = _ffn_tmajor_call(xp, *ff1, nsb=bp, tt=64)
    x2, c4p, hp, c31p = _mixer_call(
        x1, jnp.zeros((C4_W - 1, bp, D), F32), jnp.zeros((bp, D), F32),
        jnp.zeros((C31_W - 1, bp, D), F32), mix, None, nsb=bp, tt=32)
    xp_out = _attn_ffn_prompt_call(x2, mkb.reshape(bp, N_MEM, D), mvb.reshape(bp, N_MEM, D),
                                   vec('g_mem_pre'), wq, wo, vec('g_mem_post'), *ff2, tm=512)

    s1 = _ffn_tmajor_call(xs, *ff1, nsb=64, tt=ss)
    s2, q, c4s, hs, c31s = _mixer_call(
        s1, _seq_first(state_lru_conv[l]), state_lru_h[l], _seq_first(state_conf_conv[l]), mix,
        (vec('g_mem_pre'), wq), nsb=32, tt=ss)
    att = _attn_sample_call(q.reshape(bs * ss, D), cache_mem_k[l:l + 1], cache_mem_v[l:l + 1], sq=4)
    xs_out = _oproj_ffn_call(s2.reshape(bs * ss, D), att, wo, vec('g_mem_post'), *ff2,
                             tm=512).reshape(bs, ss, D)

    return (xp_out, xs_out, _seq_first(c4p), hp, _seq_first(c31p), mk5, mv5,
            _seq_first(c4s), hs, _seq_first(c31s))


def kernel(x_prompt, x_sample, mem_prompt, state_lru_conv, state_lru_h, state_conf_conv, cache_mem_k, cache_mem_v, g_ff1_pre, ff1_w_gate, ff1_w_up, ff1_w_down, g_ff1_post, g_mix_pre, w_in, lru_conv_w, lru_conv_b, lru_w_a, lru_b_a, lru_w_x, lru_b_x, lru_lambda, conf_conv_w, conf_conv_b, conf_ln_g, conf_ln_b, w_out, g_mix_post, g_mem_pre, g_mem_kv, w_mem_k, w_mem_v, w_q, w_o, g_mem_post, g_ff2_pre, ff2_w_gate, ff2_w_up, ff2_w_down, g_ff2_post):
    w = dict(g_ff1_pre=g_ff1_pre, ff1_w_gate=ff1_w_gate, ff1_w_up=ff1_w_up, ff1_w_down=ff1_w_down,
             g_ff1_post=g_ff1_post, g_mix_pre=g_mix_pre, w_in=w_in, lru_conv_w=lru_conv_w,
             lru_conv_b=lru_conv_b, lru_w_a=lru_w_a, lru_b_a=lru_b_a, lru_w_x=lru_w_x,
             lru_b_x=lru_b_x, lru_lambda=lru_lambda, conf_conv_w=conf_conv_w,
             conf_conv_b=conf_conv_b, conf_ln_g=conf_ln_g, conf_ln_b=conf_ln_b, w_out=w_out,
             g_mix_post=g_mix_post, g_mem_pre=g_mem_pre, g_mem_kv=g_mem_kv, w_mem_k=w_mem_k,
             w_mem_v=w_mem_v, w_q=w_q, w_o=w_o, g_mem_post=g_mem_post, g_ff2_pre=g_ff2_pre,
             ff2_w_gate=ff2_w_gate, ff2_w_up=ff2_w_up, ff2_w_down=ff2_w_down, g_ff2_post=g_ff2_post)
    depth = w_in.shape[0]
    xp, xs = x_prompt, x_sample
    per_layer = []
    for l in range(depth):
        outs = _layer(l, xp, xs, mem_prompt, state_lru_conv, state_lru_h, state_conf_conv,
                      cache_mem_k, cache_mem_v, w)
        xp, xs = outs[0], outs[1]
        per_layer.append(outs[2:])
    stacked = [jnp.stack([pl_[i] for pl_ in per_layer]) for i in range(8)]
    return (xp, xs, *stacked)
```

```python
import functools

import jax
import jax.numpy as jnp
from jax import lax
from jax.experimental import pallas as pl
from jax.experimental.pallas import tpu as pltpu

D = 1024
D_FF = 2816
N_PAIR = 4
PAIR_W = 256
LRU_C = 8.0
C4_W = 4
C31_W = 31
N_MEM = 256
N_HEADS = 4
HEAD_D = 256
FFN_RES = 0.5
EPS = 1e-6
SUB = 8
LANE = 128
CONV_CHUNK = 8
CONV_TAPS = 16
CONV_GROUP = 4
Z_BLOCK = 256
FF_BLOCK = 256
N_X1 = 3
VMEM_LIMIT = 60 * 1024 * 1024

BF = jnp.bfloat16
F32 = jnp.float32


def _dot(a, b):
    return jnp.dot(a, b, preferred_element_type=F32)


def _rms(x, g):
    ms = jnp.mean(x * x, axis=-1, keepdims=True)
    return x * lax.rsqrt(ms + EPS) * g


def _sigmoid(x):
    return 1.0 / (1.0 + jnp.exp(-x))


def _gelu_tanh(x):
    return 0.5 * x * (1.0 + jnp.tanh(0.7978845608028654 * (x + 0.044715 * (x * x * x))))


def _ffn(x, gpre, wg_ref, wu_ref, wd_ref, gpost):
    h = _rms(x, gpre).astype(BF)
    gate = _dot(h, wg_ref[...])
    up = _dot(h, wu_ref[...])
    act = (gate * _sigmoid(gate) * up).astype(BF)
    y = _dot(act, wd_ref[...])
    return x + FFN_RES * _rms(y, gpost)


def _gate_math(r_pre, i_pre, lam, xc):
    r = _sigmoid(r_pre)
    i = _sigmoid(i_pre)
    softplus_neg_lam = jnp.maximum(-lam, 0.0) + jnp.log1p(jnp.exp(-jnp.abs(lam)))
    log_a = (-LRU_C * r) * softplus_neg_lam
    a = jnp.exp(log_a)
    th = jnp.tanh(log_a)
    y = (-2.0 * th) / (1.0 - th)
    root = jnp.where(y == 0.0, 0.0, y * lax.rsqrt(y))
    return a, root * (i * xc)


def _layernorm(x, g, b):
    xc = x - jnp.mean(x, axis=-1, keepdims=True)
    return xc * lax.rsqrt(jnp.mean(xc * xc, axis=-1, keepdims=True) + EPS) * g + b


def _const_spec(shape):
    n = len(shape)
    return pl.BlockSpec(shape, lambda *_: (0,) * n, pipeline_mode=pl.Buffered(1))


def _params(n_grid):
    return pltpu.CompilerParams(dimension_semantics=("arbitrary",) * n_grid,
                                vmem_limit_bytes=VMEM_LIMIT)


def _ffn_tmajor_kernel(x_ref, gpre_ref, wg_ref, wu_ref, wd_ref, gpost_ref, o_ref, *, nsb, tt):
    hs = nsb // 2
    for i in range(2):
        x = x_ref[pl.ds(i * hs, hs)].reshape(hs * tt, D)
        y = _ffn(x, gpre_ref[...], wg_ref, wu_ref, wd_ref, gpost_ref[...])
        o_ref[:, pl.ds(i * hs, hs), :] = jnp.swapaxes(y.reshape(hs, tt, D), 0, 1)


def _ffn_tmajor_call(x3d, gpre, wg, wu, wd, gpost, nsb, tt):
    ns, t, _ = x3d.shape
    return pl.pallas_call(
        functools.partial(_ffn_tmajor_kernel, nsb=nsb, tt=tt),
        grid=(ns // nsb, t // tt),
        in_specs=[pl.BlockSpec((nsb, tt, D), lambda i, j: (i, j, 0)),
                  _const_spec((1, D)), _const_spec((D, D_FF)), _const_spec((D, D_FF)),
                  _const_spec((D_FF, D)), _const_spec((1, D))],
        out_specs=pl.BlockSpec((tt, nsb, D), lambda i, j: (j, i, 0)),
        out_shape=jax.ShapeDtypeStruct((t, ns, D), F32),
        compiler_params=_params(2),
        name="ffn1",
    )(x3d, gpre, wg, wu, wd, gpost)


def _mixer_kernel(*refs, nsb, tt, with_q, fused):
    refs = list(refs)
    if fused:
        x0_ref, x1in_ref, xn_ref, f_gpre_ref, wg_ref, wu_ref, wd_ref, f_gpost_ref = refs[:8]
        refs = refs[8:]
    else:
        x_ref = refs.pop(0)
    (h4_ref, h0_ref, h31_ref, gpre_ref, win_ref, c4w_ref, c4b_ref, wax_ref, ba_ref, bx_ref,
     lam_ref, c31w_ref, c31b_ref, lng_ref, lnb_ref, wout_ref, gpost_ref) = refs[:17]
    refs = refs[17:]
    if with_q:
        gq_ref, wq_ref, o_ref, q_ref = refs[:4]
        refs = refs[4:]
    else:
        o_ref = refs.pop(0)
    nc4_ref, nh_ref, nc31_ref = refs[:3]
    z_buf, zb, cb, a_buf, u_buf, c_buf, gelu_buf, h_carry, w4b, w31b = refs[3:13]
    if fused:
        x1_buf, hf_buf, hn_buf, act_buf = refs[13:]
        y_buf = gelu_buf

    g = nsb // SUB
    rows = tt * nsb
    hist4, hist31 = C4_W - 1, C31_W - 1
    n_lc = D // LANE
    n_zb = 4 * D // Z_BLOCK
    n_ff = D_FF // FF_BLOCK
    n_down = D // Z_BLOCK
    j = pl.program_id(1)

    def project_in(hb):
        for c in range(n_zb):
            z_buf[c] = _dot(hb, win_ref[c])

    def ffn_in(xb_ref):
        hf_buf[...] = _rms(xb_ref[...].reshape(rows, D), f_gpre_ref[...]).astype(BF)

    def ffn_hidden(c):
        cols = pl.ds(c * FF_BLOCK, FF_BLOCK)
        gate = _dot(hf_buf[...], wg_ref[:, cols])
        up = _dot(hf_buf[...], wu_ref[:, cols])
        act_buf[:, cols] = (gate * _sigmoid(gate) * up).astype(BF)

    def ffn_down(n):
        cols = pl.ds(n * Z_BLOCK, Z_BLOCK)
        y_buf[:, cols] = _dot(act_buf[...], wd_ref[:, cols])

    def ffn_out(xb_ref, slot):
        x1 = xb_ref[...].reshape(rows, D) + FFN_RES * _rms(y_buf[...], f_gpost_ref[...])
        x1 = jnp.swapaxes(x1.reshape(nsb, tt, D), 0, 1).reshape(rows, D)
        x1_buf[slot] = x1
        hn_buf[...] = _rms(x1, gpre_ref[...]).astype(BF)

    @pl.when((pl.program_id(0) == 0) & (j == 0))
    def _():
        for k in range(C4_W):
            w4b[k] = jnp.broadcast_to(c4w_ref[pl.ds(k, 1), :], (SUB, D))
        for k in range(C31_W):
            for lc in range(n_lc):
                w31b[lc, k] = jnp.broadcast_to(c31w_ref[pl.ds(k, 1), pl.ds(lc * LANE, LANE)],
                                               (SUB, LANE))
        for lc in range(n_lc):
            w31b[lc, C31_W] = jnp.broadcast_to(c31b_ref[:, pl.ds(lc * LANE, LANE)], (SUB, LANE))

    @pl.when(j == 0)
    def _():
        zb[pl.ds(0, hist4)] = h4_ref[...].reshape(hist4, g, SUB, D)
        for lc in range(n_lc):
            cb[lc, pl.ds(0, hist31)] = h31_ref[:, :, pl.ds(lc * LANE, LANE)].reshape(
                hist31, g, SUB, LANE)
        h_carry[...] = h0_ref[...].reshape(g, SUB, D)
        if fused:
            for slot, xb_ref in enumerate((x0_ref, x1in_ref)):
                ffn_in(xb_ref)
                for c in range(n_ff):
                    ffn_hidden(c)
                for n in range(n_down):
                    ffn_down(n)
                ffn_out(xb_ref, slot)
                if slot == 0:
                    project_in(hn_buf[...])

    if fused:
        x = x1_buf[j % N_X1]
        ffn_in(xn_ref)
    else:
        x = x_ref[...].reshape(rows, D)
        project_in(_rms(x, gpre_ref[...]).astype(BF))

    def z_cols(part, lo, width):
        b0 = (part * D + lo) // Z_BLOCK
        return jnp.concatenate([z_buf[b0 + i] for i in range(width // Z_BLOCK)], axis=-1)

    zb[pl.ds(hist4, tt)] = z_cols(0, 0, D).reshape(tt, g, SUB, D)
    xc = jnp.broadcast_to(c4b_ref[...].reshape(1, 1, 1, D), (tt, g, SUB, D))
    for k in range(C4_W):
        xc = xc + w4b[k][None, None] * zb[pl.ds(k, tt)]
    zb[pl.ds(0, hist4)] = zb[pl.ds(tt, hist4)]

    xcb = xc.reshape(rows, D).astype(BF)
    xc2 = xc.reshape(rows, D)
    ff_per_pair = -(-n_ff // N_PAIR)
    for p in range(N_PAIR):
        if fused:
            for c in range(p * ff_per_pair, min(n_ff, (p + 1) * ff_per_pair)):
                ffn_hidden(c)
        sl = slice(p * PAIR_W, (p + 1) * PAIR_W)
        g2 = _dot(xcb[:, sl], wax_ref[p])
        a, u = _gate_math(g2[:, :PAIR_W] + ba_ref[:, sl], g2[:, PAIR_W:] + bx_ref[:, sl],
                          lam_ref[:, sl], xc2[:, sl])
        a_buf[:, :, :, sl] = a.reshape(tt, g, SUB, PAIR_W)
        u_buf[:, :, :, sl] = u.reshape(tt, g, SUB, PAIR_W)
        gelu_buf[:, sl] = _gelu_tanh(z_cols(1, p * PAIR_W, PAIR_W))
        glu = (z_cols(2, p * PAIR_W, PAIR_W) * _sigmoid(z_cols(3, p * PAIR_W, PAIR_W))).reshape(
            tt, g, SUB, PAIR_W)
        for i_ in range(PAIR_W // LANE):
            cb[p * (PAIR_W // LANE) + i_, pl.ds(hist31, tt)] = glu[..., i_ * LANE:(i_ + 1) * LANE]

    def scan_body(t, hc):
        hc = a_buf[t] * hc + u_buf[t]
        u_buf[t] = hc
        return hc

    h_carry[...] = lax.fori_loop(0, tt, scan_body, h_carry[...])

    rb = min(tt, CONV_CHUNK)

    def conv_col(lc):
        for k0 in range(0, C31_W, CONV_TAPS):
            nk = min(CONV_TAPS, C31_W - k0)
            wk = [w31b[lc, k0 + i] for i in range(nk)]
            for gi in range(g):
                for t0 in range(0, tt, rb):
                    if k0 == 0:
                        acc = [w31b[lc, C31_W] for _ in range(rb)]
                    else:
                        acc = [c_buf[lc, t0 + r, gi] for r in range(rb)]
                    for s in range(rb + nk - 1):
                        xs = cb[lc, t0 + k0 + s, gi]
                        for r in range(rb):
                            if 0 <= s - r < nk:
                                acc[r] = acc[r] + wk[s - r] * xs
                    for r in range(rb):
                        c_buf[lc, t0 + r, gi] = acc[r]

    if fused:
        per = n_zb // n_lc

        def conv_group(it, carry):
            for k in range(CONV_GROUP):
                lc = it * CONV_GROUP + k
                for i_ in range(per):
                    z_buf[lc * per + i_] = _dot(hn_buf[...], win_ref[lc * per + i_])
                conv_col(lc)
            return carry

        lax.fori_loop(0, n_lc // CONV_GROUP, conv_group, 0)
    else:
        lax.fori_loop(0, n_lc, lambda lc, c_: (conv_col(lc), c_)[1], 0)

    for lc in range(n_lc):
        for s in range(hist31):
            cb[lc, s] = cb[lc, tt + s]

    y_lru = (gelu_buf[...] * u_buf[...].reshape(rows, D)).astype(BF)
    if fused:
        ffn_down(0)
    out_lru = _dot(y_lru, wout_ref[pl.ds(0, D), :])
    c = jnp.concatenate([c_buf[lc].reshape(rows, LANE) for lc in range(n_lc)], axis=-1)
    yc = _layernorm(c, lng_ref[...], lnb_ref[...])
    if fused:
        ffn_down(1)
    y_conf = (yc * _sigmoid(yc)).astype(BF)
    th_ = tt // 2
    down_rest = (n_down - 2) // 2
    for hf in range(2):
        rs = slice(hf * th_ * nsb, (hf + 1) * th_ * nsb)
        out = out_lru[rs] + _dot(y_conf[rs], wout_ref[pl.ds(D, D), :])
        if fused:
            for n in range(2 + hf * down_rest, 2 + (hf + 1) * down_rest):
                ffn_down(n)
        x2 = x[rs] + _rms(out, gpost_ref[...])
        o_ref[:, pl.ds(hf * th_, th_), :] = jnp.swapaxes(x2.reshape(th_, nsb, D), 0, 1)
        if with_q:
            q = _dot(_rms(x2, gq_ref[...]).astype(BF), wq_ref[...])
            q_ref[:, pl.ds(hf * th_, th_), :] = jnp.swapaxes(q.reshape(th_, nsb, D), 0, 1)
    if fused:
        ffn_out(xn_ref, (j + 2) % N_X1)

    @pl.when(j == pl.num_programs(1) - 1)
    def _():
        nc4_ref[...] = zb[pl.ds(0, hist4)].reshape(hist4, nsb, D)
        nh_ref[...] = h_carry[...].reshape(nsb, D)
        for lc in range(n_lc):
            nc31_ref[:, :, pl.ds(lc * LANE, LANE)] = cb[lc, pl.ds(0, hist31)].reshape(
                hist31, nsb, LANE)


def _mixer_call(x, hist4, h0, hist31, mix, q_params, ffn_params, nsb, tt):
    fused = ffn_params is not None
    if fused:
        ns, t, _ = x.shape
    else:
        t, ns, _ = x.shape
    n_t = t // tt
    with_q = q_params is not None
    vec = _const_spec((1, D))
    seq_major = pl.BlockSpec((nsb, tt, D), lambda i, j: (i, j, 0))
    if fused:
        assert n_t >= 2 and (D // Z_BLOCK) % 2 == 0
        first = lambda k: pl.BlockSpec((nsb, tt, D), lambda i, j: (i, k, 0),
                                       pipeline_mode=pl.Buffered(1))
        in_specs = [first(0), first(1),
                    pl.BlockSpec((nsb, tt, D), lambda i, j: (i, jnp.minimum(j + 2, n_t - 1), 0)),
                    vec, _const_spec((D, D_FF)), _const_spec((D, D_FF)), _const_spec((D_FF, D)), vec]
        args = [x, x, x, *ffn_params]
    else:
        in_specs = [pl.BlockSpec((tt, nsb, D), lambda i, j: (j, i, 0))]
        args = [x]
    in_specs += [pl.BlockSpec((C4_W - 1, nsb, D), lambda i, j: (0, i, 0)),
                 pl.BlockSpec((nsb, D), lambda i, j: (i, 0)),
                 pl.BlockSpec((C31_W - 1, nsb, D), lambda i, j: (0, i, 0)),
                 vec, _const_spec((4 * D // Z_BLOCK, D, Z_BLOCK)), _const_spec((C4_W, D)), vec,
                 _const_spec((N_PAIR, PAIR_W, 2 * PAIR_W)), vec, vec, vec,
                 _const_spec((C31_W, D)), vec, vec, vec, _const_spec((2 * D, D)), vec]
    args += [hist4, h0, hist31, *mix]
    out_specs = [seq_major]
    out_shape = [jax.ShapeDtypeStruct((ns, t, D), F32)]
    if with_q:
        in_specs += [vec, _const_spec((D, D))]
        out_specs.append(seq_major)
        out_shape.append(jax.ShapeDtypeStruct((ns, t, D), F32))
        args += list(q_params)
    out_specs += [pl.BlockSpec((C4_W - 1, nsb, D), lambda i, j: (0, i, 0)),
                  pl.BlockSpec((nsb, D), lambda i, j: (i, 0)),
                  pl.BlockSpec((C31_W - 1, nsb, D), lambda i, j: (0, i, 0))]
    out_shape += [jax.ShapeDtypeStruct((C4_W - 1, ns, D), F32),
                  jax.ShapeDtypeStruct((ns, D), F32),
                  jax.ShapeDtypeStruct((C31_W - 1, ns, D), F32)]
    g = nsb // SUB
    rows = tt * nsb
    tile = lambda n: pltpu.VMEM((n, g, SUB, D), F32)
    col = lambda n: pltpu.VMEM((D // LANE, n, g, SUB, LANE), F32)
    scratch = [pltpu.VMEM((4 * D // Z_BLOCK, rows, Z_BLOCK), F32),
               tile(C4_W - 1 + tt), col(C31_W - 1 + tt), tile(tt), tile(tt), col(tt),
               pltpu.VMEM((rows, D), F32), pltpu.VMEM((g, SUB, D), F32),
               pltpu.VMEM((C4_W, SUB, D), F32),
               pltpu.VMEM((D // LANE, C31_W + 1, SUB, LANE), F32)]
    if fused:
        scratch += [pltpu.VMEM((N_X1, rows, D), F32), pltpu.VMEM((rows, D), BF),
                    pltpu.VMEM((rows, D), BF), pltpu.VMEM((rows, D_FF), BF)]
    return pl.pallas_call(
        functools.partial(_mixer_kernel, nsb=nsb, tt=tt, with_q=with_q, fused=fused),
        grid=(ns // nsb, n_t),
        in_specs=in_specs, out_specs=out_specs, out_shape=out_shape,
        scratch_shapes=scratch,
        compiler_params=_params(2),
        name="ffn1_mixer" if fused else "mixer",
    )(*args)


def _memkv_kernel(m_ref, g_ref, wk_ref, wv_ref, k5_ref, v5_ref, kb_ref, vb_ref, *, nb):
    m = _rms(m_ref[...], g_ref[...]).astype(BF)
    k = _dot(m, wk_ref[...])
    v = _dot(m, wv_ref[...])
    k5_ref[...] = k.reshape(nb, N_MEM, N_HEADS, HEAD_D)
    v5_ref[...] = v.reshape(nb, N_MEM, N_HEADS, HEAD_D)
    kb_ref[...] = k.astype(BF)
    vb_ref[...] = v.astype(BF)


def _memkv_call(mem2d, g, wk, wv, nb):
    rows = mem2d.shape[0]
    b = rows // N_MEM
    tm = nb * N_MEM
    kv5 = pl.BlockSpec((nb, N_MEM, N_HEADS, HEAD_D), lambda i: (i, 0, 0, 0))
    row = pl.BlockSpec((tm, D), lambda i: (i, 0))
    return pl.pallas_call(
        functools.partial(_memkv_kernel, nb=nb),
        grid=(b // nb,),
        in_specs=[row, _const_spec((1, D)), _const_spec((D, D)), _const_spec((D, D))],
        out_specs=[kv5, kv5, row, row],
        out_shape=[jax.ShapeDtypeStruct((b, N_MEM, N_HEADS, HEAD_D), F32),
                   jax.ShapeDtypeStruct((b, N_MEM, N_HEADS, HEAD_D), F32),
                   jax.ShapeDtypeStruct((rows, D), BF), jax.ShapeDtypeStruct((rows, D), BF)],
        compiler_params=_params(1),
        name="memkv",
    )(mem2d, g, wk, wv)


def _attn_ffn_prompt_kernel(x_ref, k_ref, v_ref, gq_ref, wq_ref, wo_ref, gao_ref,
                            gpre_ref, wg_ref, wu_ref, wd_ref, gpost_ref, o_ref):
    x = x_ref[...]
    q = _dot(_rms(x, gq_ref[...]).astype(BF), wq_ref[...])
    k = k_ref[...]
    v = v_ref[...]
    outs = []
    for hh in range(N_HEADS):
        sl = slice(hh * HEAD_D, (hh + 1) * HEAD_D)
        s = lax.dot_general(q[:, sl].astype(BF), k[:, sl], (((1,), (1,)), ((), ())),
                            preferred_element_type=F32) * (HEAD_D ** -0.5)
        e = jnp.exp(s - jnp.max(s, axis=-1, keepdims=True))
        p = e / jnp.sum(e, axis=-1, keepdims=True)
        outs.append(_dot(p.astype(BF), v[:, sl]))
    o = jnp.concatenate(outs, axis=-1)
    x3 = x + _rms(_dot(o.astype(BF), wo_ref[...]), gao_ref[...])
    o_ref[...] = _ffn(x3, gpre_ref[...], wg_ref, wu_ref, wd_ref, gpost_ref[...])


def _attn_ffn_prompt_call(x3d, k3d, v3d, gq, wq, wo, gao, gpre, wg, wu, wd, gpost, tm):
    b, s, _ = x3d.shape
    vec = _const_spec((1, D))
    return pl.pallas_call(
        _attn_ffn_prompt_kernel,
        grid=(b, s // tm),
        in_specs=[pl.BlockSpec((None, tm, D), lambda i, j: (i, j, 0)),
                  pl.BlockSpec((None, N_MEM, D), lambda i, j: (i, 0, 0)),
                  pl.BlockSpec((None, N_MEM, D), lambda i, j: (i, 0, 0)),
                  vec, _const_spec((D, D)), _const_spec((D, D)), vec,
                  vec, _const_spec((D, D_FF)), _const_spec((D, D_FF)), _const_spec((D_FF, D)), vec],
        out_specs=pl.BlockSpec((None, tm, D), lambda i, j: (i, j, 0)),
        out_shape=jax.ShapeDtypeStruct((b, s, D), F32),
        compiler_params=_params(2),
        name="attn_ffn_prompt",
    )(x3d, k3d, v3d, gq, wq, wo, gao, gpre, wg, wu, wd, gpost)


def _attn_sample_kernel(q_ref, k_ref, v_ref, o_ref, *, sq):
    rows = N_HEADS * SUB
    cols = N_MEM * N_HEADS
    own_head = (lax.broadcasted_iota(jnp.int32, (rows, cols), 1) % N_HEADS
                == lax.broadcasted_iota(jnp.int32, (rows, cols), 0) // SUB)
    for s in range(sq):
        kn = k_ref[s].reshape(cols, HEAD_D).astype(BF)
        vn = v_ref[s].reshape(cols, HEAD_D).astype(BF)
        q = q_ref[pl.ds(s * SUB, SUB), :]
        qh = jnp.concatenate([q[:, h * HEAD_D:(h + 1) * HEAD_D] for h in range(N_HEADS)], axis=0)
        sc = lax.dot_general(qh.astype(BF), kn, (((1,), (1,)), ((), ())),
                             preferred_element_type=F32) * (HEAD_D ** -0.5)
        sc = jnp.where(own_head, sc, -1e30)
        e = jnp.exp(sc - jnp.max(sc, axis=-1, keepdims=True))
        p = e / jnp.sum(e, axis=-1, keepdims=True)
        oh = _dot(p.astype(BF), vn)
        for h in range(N_HEADS):
            o_ref[pl.ds(s * SUB, SUB), pl.ds(h * HEAD_D, HEAD_D)] = oh[h * SUB:(h + 1) * SUB, :]


def _attn_sample_call(q2d, k5, v5, sq):
    n_seq = k5.shape[1]
    rows = sq * SUB
    kv = pl.BlockSpec((None, sq, N_MEM, N_HEADS, HEAD_D), lambda i: (0, i, 0, 0, 0))
    return pl.pallas_call(
        functools.partial(_attn_sample_kernel, sq=sq),
        grid=(n_seq // sq,),
        in_specs=[pl.BlockSpec((rows, D), lambda i: (i, 0)), kv, kv],
        out_specs=pl.BlockSpec((rows, D), lambda i: (i, 0)),
        out_shape=jax.ShapeDtypeStruct((n_seq * SUB, D), F32),
        compiler_params=_params(1),
        name="attn_sample",
    )(q2d, k5, v5)


def _oproj_ffn_kernel(x_ref, a_ref, wo_ref, gao_ref, gpre_ref, wg_ref, wu_ref, wd_ref, gpost_ref,
                      o_ref):
    x3 = x_ref[...] + _rms(_dot(a_ref[...].astype(BF), wo_ref[...]), gao_ref[...])
    o_ref[...] = _ffn(x3, gpre_ref[...], wg_ref, wu_ref, wd_ref, gpost_ref[...])


def _oproj_ffn_call(x2d, a2d, wo, gao, gpre, wg, wu, wd, gpost, tm):
    rows = x2d.shape[0]
    vec = _const_spec((1, D))
    row = pl.BlockSpec((tm, D), lambda i: (i, 0))
    return pl.pallas_call(
        _oproj_ffn_kernel,
        grid=(rows // tm,),
        in_specs=[row, row, _const_spec((D, D)), vec,
                  vec, _const_spec((D, D_FF)), _const_spec((D, D_FF)), _const_spec((D_FF, D)), vec],
        out_specs=row,
        out_shape=jax.ShapeDtypeStruct((rows, D), F32),
        compiler_params=_params(1),
        name="oproj_ffn",
    )(x2d, a2d, wo, gao, gpre, wg, wu, wd, gpost)


def _pair_gate_weights(w_a, w_x):
    hd = w_a.shape[-1]

    def blockdiag(w):
        w = w.reshape(N_PAIR, 2, hd, hd)
        z = jnp.zeros((N_PAIR, hd, hd), w.dtype)
        top = jnp.concatenate([w[:, 0], z], axis=-1)
        bot = jnp.concatenate([z, w[:, 1]], axis=-1)
        return jnp.concatenate([top, bot], axis=-2)

    return jnp.concatenate([blockdiag(w_a), blockdiag(w_x)], axis=-1).astype(BF)


def _column_blocks(w_in):
    n = 4 * D // Z_BLOCK
    return jnp.swapaxes(w_in.reshape(D, n, Z_BLOCK), 0, 1)


def _seq_first(a):
    return jnp.swapaxes(a, 0, 1)


def _layer(l, xp, xs, mem_prompt, state_lru_conv, state_lru_h, state_conf_conv, cache_mem_k,
           cache_mem_v, w):
    bp, sp, _ = xp.shape
    bs, ss, _ = xs.shape
    vec = lambda name: w[name][l].reshape(1, D)
    bf = lambda name: w[name][l].astype(BF)

    ff1 = (vec('g_ff1_pre'), bf('ff1_w_gate'), bf('ff1_w_up'), bf('ff1_w_down'), vec('g_ff1_post'))
    ff2 = (vec('g_ff2_pre'), bf('ff2_w_gate'), bf('ff2_w_up'), bf('ff2_w_down'), vec('g_ff2_post'))
    mix = (vec('g_mix_pre'), _column_blocks(bf('w_in')), w['lru_conv_w'][l], vec('lru_conv_b'),
           _pair_gate_weights(w['lru_w_a'][l], w['lru_w_x'][l]), vec('lru_b_a'), vec('lru_b_x'),
           vec('lru_lambda'), w['conf_conv_w'][l], vec('conf_conv_b'), vec('conf_ln_g'),
           vec('conf_ln_b'), bf('w_out'), vec('g_mix_post'))
    wq, wo = bf('w_q'), bf('w_o')

    mk5, mv5, mkb, mvb = _memkv_call(mem_prompt.reshape(bp * N_MEM, D), vec('g_mem_kv'),
                                     bf('w_mem_k'), bf('w_mem_v'), nb=2)
    x2, c4p, hp, c31p = _mixer_call(
        xp, jnp.zeros((C4_W - 1, bp, D), F32), jnp.zeros((bp, D), F32),
        jnp.zeros((C31_W - 1, bp, D), F32), mix, None, ff1, nsb=bp, tt=32)
    xp_out = _attn_ffn_prompt_call(x2, mkb.reshape(bp, N_MEM, D), mvb.reshape(bp, N_MEM, D),
                                   vec('g_mem_pre'), wq, wo, vec('g_mem_post'), *ff2, tm=512)

    s1 = _ffn_tmajor_call(xs, *ff1, nsb=64, tt=ss)
    s2, q, c4s, hs, c31s = _mixer_call(
        s1, _seq_first(state_lru_conv[l]), state_lru_h[l], _seq_first(state_conf_conv[l]), mix,
        (vec('g_mem_pre'), wq), None, nsb=32, tt=ss)
    att = _attn_sample_call(q.reshape(bs * ss, D), cache_mem_k[l:l + 1], cache_mem_v[l:l + 1], sq=4)
    xs_out = _oproj_ffn_call(s2.reshape(bs * ss, D), att, wo, vec('g_mem_post'), *ff2,
                             tm=512).reshape(bs, ss, D)

    return (xp_out, xs_out, _seq_first(c4p), hp, _seq_first(c31p), mk5, mv5,
            _seq_first(c4s), hs, _seq_first(c31s))


def kernel(x_prompt, x_sample, mem_prompt, state_lru_conv, state_lru_h, state_conf_conv, cache_mem_k, cache_mem_v, g_ff1_pre, ff1_w_gate, ff1_w_up, ff1_w_down, g_ff1_post, g_mix_pre, w_in, lru_conv_w, lru_conv_b, lru_w_a, lru_b_a, lru_w_x, lru_b_x, lru_lambda, conf_conv_w, conf_conv_b, conf_ln_g, conf_ln_b, w_out, g_mix_post, g_mem_pre, g_mem_kv, w_mem_k, w_mem_v, w_q, w_o, g_mem_post, g_ff2_pre, ff2_w_gate, ff2_w_up, ff2_w_down, g_ff2_post):
    w = dict(g_ff1_pre=g_ff1_pre, ff1_w_gate=ff1_w_gate, ff1_w_up=ff1_w_up, ff1_w_down=ff1_w_down,
             g_ff1_post=g_ff1_post, g_mix_pre=g_mix_pre, w_in=w_in, lru_conv_w=lru_conv_w,
             lru_conv_b=lru_conv_b, lru_w_a=lru_w_a, lru_b_a=lru_b_a, lru_w_x=lru_w_x,
             lru_b_x=lru_b_x, lru_lambda=lru_lambda, conf_conv_w=conf_conv_w,
             conf_conv_b=conf_conv_b, conf_ln_g=conf_ln_g, conf_ln_b=conf_ln_b, w_out=w_out,
             g_mix_post=g_mix_post, g_mem_pre=g_mem_pre, g_mem_kv=g_mem_kv, w_mem_k=w_mem_k,
             w_mem_v=w_mem_v, w_q=w_q, w_o=w_o, g_mem_post=g_mem_post, g_ff2_pre=g_ff2_pre,
             ff2_w_gate=ff2_w_gate, ff2_w_up=ff2_w_up, ff2_w_down=ff2_w_down, g_ff2_post=g_ff2_post)
    depth = w_in.shape[0]
    xp, xs = x_prompt, x_sample
    per_layer = []
    for l in range(depth):
        outs = _layer(l, xp, xs, mem_prompt, state_lru_conv, state_lru_h, state_conf_conv,
                      cache_mem_k, cache_mem_v, w)
        xp, xs = outs[0], outs[1]
        per_layer.append(outs[2:])
    stacked = [jnp.stack([pl_[i] for pl_ in per_layer]) for i in range(8)]
    return (xp, xs, *stacked)
```

```python
import functools

import jax
import jax.numpy as jnp
from jax import lax
from jax.experimental import pallas as pl
from jax.experimental.pallas import tpu as pltpu

D = 1024
D_FF = 2816
N_PAIR = 4
PAIR_W = 256
LRU_C = 8.0
C4_W = 4
C31_W = 31
N_MEM = 256
N_HEADS = 4
HEAD_D = 256
FFN_RES = 0.5
EPS = 1e-6
SUB = 8
LANE = 128
CONV_CHUNK = 8
CONV_TAPS = 16
CONV_GROUP = 4
Z_BLOCK = 256
FF_BLOCK = 256
N_X1 = 3
VMEM_LIMIT = 60 * 1024 * 1024

BF = jnp.bfloat16
F32 = jnp.float32


def _dot(a, b):
    return jnp.dot(a, b, preferred_element_type=F32)


def _rms(x, g):
    ms = jnp.mean(x * x, axis=-1, keepdims=True)
    return x * lax.rsqrt(ms + EPS) * g


def _sigmoid(x):
    return 1.0 / (1.0 + jnp.exp(-x))


def _gelu_tanh(x):
    return 0.5 * x * (1.0 + jnp.tanh(0.7978845608028654 * (x + 0.044715 * (x * x * x))))


def _ffn(x, gpre, wg_ref, wu_ref, wd_ref, gpost):
    h = _rms(x, gpre).astype(BF)
    gate = _dot(h, wg_ref[...])
    up = _dot(h, wu_ref[...])
    act = (gate * _sigmoid(gate) * up).astype(BF)
    y = _dot(act, wd_ref[...])
    return x + FFN_RES * _rms(y, gpost)


def _gate_math(r_pre, i_pre, lam, xc):
    r = _sigmoid(r_pre)
    i = _sigmoid(i_pre)
    softplus_neg_lam = jnp.maximum(-lam, 0.0) + jnp.log1p(jnp.exp(-jnp.abs(lam)))
    log_a = (-LRU_C * r) * softplus_neg_lam
    a = jnp.exp(log_a)
    th = jnp.tanh(log_a)
    y = (-2.0 * th) / (1.0 - th)
    root = jnp.where(y == 0.0, 0.0, y * lax.rsqrt(y))
    return a, root * (i * xc)


def _layernorm(x, g, b):
    xc = x - jnp.mean(x, axis=-1, keepdims=True)
    return xc * lax.rsqrt(jnp.mean(xc * xc, axis=-1, keepdims=True) + EPS) * g + b


def _const_spec(shape):
    n = len(shape)
    return pl.BlockSpec(shape, lambda *_: (0,) * n, pipeline_mode=pl.Buffered(1))


def _params(n_grid):
    return pltpu.CompilerParams(dimension_semantics=("arbitrary",) * n_grid,
                                vmem_limit_bytes=VMEM_LIMIT)


def _ffn_tmajor_kernel(x_ref, gpre_ref, wg_ref, wu_ref, wd_ref, gpost_ref, o_ref, *, nsb, tt):
    hs = nsb // 2
    for i in range(2):
        x = x_ref[pl.ds(i * hs, hs)].reshape(hs * tt, D)
        y = _ffn(x, gpre_ref[...], wg_ref, wu_ref, wd_ref, gpost_ref[...])
        o_ref[:, pl.ds(i * hs, hs), :] = jnp.swapaxes(y.reshape(hs, tt, D), 0, 1)


def _ffn_tmajor_call(x3d, gpre, wg, wu, wd, gpost, nsb, tt):
    ns, t, _ = x3d.shape
    return pl.pallas_call(
        functools.partial(_ffn_tmajor_kernel, nsb=nsb, tt=tt),
        grid=(ns // nsb, t // tt),
        in_specs=[pl.BlockSpec((nsb, tt, D), lambda i, j: (i, j, 0)),
                  _const_spec((1, D)), _const_spec((D, D_FF)), _const_spec((D, D_FF)),
                  _const_spec((D_FF, D)), _const_spec((1, D))],
        out_specs=pl.BlockSpec((tt, nsb, D), lambda i, j: (j, i, 0)),
        out_shape=jax.ShapeDtypeStruct((t, ns, D), F32),
        compiler_params=_params(2),
        name="ffn1",
    )(x3d, gpre, wg, wu, wd, gpost)


def _mixer_kernel(*refs, nsb, tt, with_q, fused):
    refs = list(refs)
    if fused:
        x01_ref, xn_ref, f_gpre_ref, wg_ref, wu_ref, wd_ref, f_gpost_ref = refs[:7]
        refs = refs[7:]
    else:
        x_ref = refs.pop(0)
    (h4_ref, h0_ref, h31_ref, gpre_ref, win_ref, c4w_ref, c4b_ref, wax_ref, ba_ref, bx_ref,
     lam_ref, c31w_ref, c31b_ref, lng_ref, lnb_ref, wout_ref, gpost_ref) = refs[:17]
    refs = refs[17:]
    if with_q:
        gq_ref, wq_ref, o_ref, q_ref = refs[:4]
        refs = refs[4:]
    else:
        o_ref = refs.pop(0)
    nc4_ref, nh_ref, nc31_ref = refs[:3]
    z_buf, zb, cb, a_buf, u_buf, c_buf, gelu_buf, h_carry, w4b, w31b = refs[3:13]
    if fused:
        x1_buf, hf_buf, hn_buf, act_buf = refs[13:]
        y_buf = gelu_buf

    g = nsb // SUB
    rows = tt * nsb
    hist4, hist31 = C4_W - 1, C31_W - 1
    n_lc = D // LANE
    n_zb = 4 * D // Z_BLOCK
    n_ff = D_FF // FF_BLOCK
    n_down = D // Z_BLOCK
    j = pl.program_id(1)

    def project_in(hb):
        for c in range(n_zb):
            z_buf[c] = _dot(hb, win_ref[c])

    def ffn_in(xb_ref):
        hf_buf[...] = _rms(xb_ref[...].reshape(rows, D), f_gpre_ref[...]).astype(BF)

    def ffn_hidden(c):
        gate = _dot(hf_buf[...], wg_ref[c])
        up = _dot(hf_buf[...], wu_ref[c])
        act_buf[c] = (gate * _sigmoid(gate) * up).astype(BF)

    def ffn_down(n):
        cols = pl.ds(n * Z_BLOCK, Z_BLOCK)
        act = jnp.concatenate([act_buf[c] for c in range(n_ff)], axis=-1)
        y_buf[:, cols] = _dot(act, wd_ref[:, cols])

    def ffn_out(xb_ref, slot):
        x1 = xb_ref[...].reshape(rows, D) + FFN_RES * _rms(y_buf[...], f_gpost_ref[...])
        x1 = jnp.swapaxes(x1.reshape(nsb, tt, D), 0, 1).reshape(rows, D)
        x1_buf[slot] = x1
        hn_buf[...] = _rms(x1, gpre_ref[...]).astype(BF)

    @pl.when((pl.program_id(0) == 0) & (j == 0))
    def _():
        for k in range(C4_W):
            w4b[k] = jnp.broadcast_to(c4w_ref[pl.ds(k, 1), :], (SUB, D))
        for k in range(C31_W):
            for lc in range(n_lc):
                w31b[lc, k] = jnp.broadcast_to(c31w_ref[pl.ds(k, 1), pl.ds(lc * LANE, LANE)],
                                               (SUB, LANE))
        for lc in range(n_lc):
            w31b[lc, C31_W] = jnp.broadcast_to(c31b_ref[:, pl.ds(lc * LANE, LANE)], (SUB, LANE))

    @pl.when(j == 0)
    def _():
        zb[pl.ds(0, hist4)] = h4_ref[...].reshape(hist4, g, SUB, D)
        for lc in range(n_lc):
            cb[lc, pl.ds(0, hist31)] = h31_ref[:, :, pl.ds(lc * LANE, LANE)].reshape(
                hist31, g, SUB, LANE)
        h_carry[...] = h0_ref[...].reshape(g, SUB, D)
        if fused:
            def hidden_body(c, carry):
                ffn_hidden(c)
                return carry

            def project_body(c, carry):
                z_buf[c] = _dot(hn_buf[...], win_ref[c])
                return carry

            def tile_body(slot, carry):
                xb = x01_ref.at[:, pl.ds(pl.multiple_of(slot * tt, tt), tt), :]
                ffn_in(xb)
                lax.fori_loop(0, n_ff, hidden_body, 0)
                for n in range(n_down):
                    ffn_down(n)
                ffn_out(xb, slot)

                @pl.when(slot == 0)
                def _():
                    lax.fori_loop(0, n_zb, project_body, 0)
                return carry

            lax.fori_loop(0, 2, tile_body, 0)

    if fused:
        x = x1_buf[j % N_X1]
        ffn_in(xn_ref)
    else:
        x = x_ref[...].reshape(rows, D)
        project_in(_rms(x, gpre_ref[...]).astype(BF))

    def z_cols(part, lo, width):
        b0 = (part * D + lo) // Z_BLOCK
        return jnp.concatenate([z_buf[b0 + i] for i in range(width // Z_BLOCK)], axis=-1)

    zb[pl.ds(hist4, tt)] = z_cols(0, 0, D).reshape(tt, g, SUB, D)
    xc = jnp.broadcast_to(c4b_ref[...].reshape(1, 1, 1, D), (tt, g, SUB, D))
    for k in range(C4_W):
        xc = xc + w4b[k][None, None] * zb[pl.ds(k, tt)]
    zb[pl.ds(0, hist4)] = zb[pl.ds(tt, hist4)]

    xcb = xc.reshape(rows, D).astype(BF)
    xc2 = xc.reshape(rows, D)
    ff_per_pair = -(-n_ff // N_PAIR)
    for p in range(N_PAIR):
        if fused:
            for c in range(p * ff_per_pair, min(n_ff, (p + 1) * ff_per_pair)):
                ffn_hidden(c)
        sl = slice(p * PAIR_W, (p + 1) * PAIR_W)
        g2 = _dot(xcb[:, sl], wax_ref[p])
        a, u = _gate_math(g2[:, :PAIR_W] + ba_ref[:, sl], g2[:, PAIR_W:] + bx_ref[:, sl],
                          lam_ref[:, sl], xc2[:, sl])
        a_buf[:, :, :, sl] = a.reshape(tt, g, SUB, PAIR_W)
        u_buf[:, :, :, sl] = u.reshape(tt, g, SUB, PAIR_W)
        gelu_buf[:, sl] = _gelu_tanh(z_cols(1, p * PAIR_W, PAIR_W))
        glu = (z_cols(2, p * PAIR_W, PAIR_W) * _sigmoid(z_cols(3, p * PAIR_W, PAIR_W))).reshape(
            tt, g, SUB, PAIR_W)
        for i_ in range(PAIR_W // LANE):
            cb[p * (PAIR_W // LANE) + i_, pl.ds(hist31, tt)] = glu[..., i_ * LANE:(i_ + 1) * LANE]

    def scan_body(t, hc):
        hc = a_buf[t] * hc + u_buf[t]
        u_buf[t] = hc
        return hc

    h_carry[...] = lax.fori_loop(0, tt, scan_body, h_carry[...])

    rb = min(tt, CONV_CHUNK)

    def conv_col(lc):
        for k0 in range(0, C31_W, CONV_TAPS):
            nk = min(CONV_TAPS, C31_W - k0)
            wk = [w31b[lc, k0 + i] for i in range(nk)]
            for gi in range(g):
                for t0 in range(0, tt, rb):
                    if k0 == 0:
                        acc = [w31b[lc, C31_W] for _ in range(rb)]
                    else:
                        acc = [c_buf[lc, t0 + r, gi] for r in range(rb)]
                    for s in range(rb + nk - 1):
                        xs = cb[lc, t0 + k0 + s, gi]
                        for r in range(rb):
                            if 0 <= s - r < nk:
                                acc[r] = acc[r] + wk[s - r] * xs
                    for r in range(rb):
                        c_buf[lc, t0 + r, gi] = acc[r]

    if fused:
        per = n_zb // n_lc

        def conv_group(it, carry):
            for k in range(CONV_GROUP):
                lc = it * CONV_GROUP + k
                for i_ in range(per):
                    z_buf[lc * per + i_] = _dot(hn_buf[...], win_ref[lc * per + i_])
                conv_col(lc)
            return carry

        lax.fori_loop(0, n_lc // CONV_GROUP, conv_group, 0)
    else:
        lax.fori_loop(0, n_lc, lambda lc, c_: (conv_col(lc), c_)[1], 0)

    for lc in range(n_lc):
        for s in range(hist31):
            cb[lc, s] = cb[lc, tt + s]

    y_lru = (gelu_buf[...] * u_buf[...].reshape(rows, D)).astype(BF)
    if fused:
        ffn_down(0)
    out_lru = _dot(y_lru, wout_ref[pl.ds(0, D), :])
    c = jnp.concatenate([c_buf[lc].reshape(rows, LANE) for lc in range(n_lc)], axis=-1)
    yc = _layernorm(c, lng_ref[...], lnb_ref[...])
    if fused:
        ffn_down(1)
    y_conf = (yc * _sigmoid(yc)).astype(BF)
    th_ = tt // 2
    down_rest = (n_down - 2) // 2
    for hf in range(2):
        rs = slice(hf * th_ * nsb, (hf + 1) * th_ * nsb)
        out = out_lru[rs] + _dot(y_conf[rs], wout_ref[pl.ds(D, D), :])
        if fused:
            for n in range(2 + hf * down_rest, 2 + (hf + 1) * down_rest):
                ffn_down(n)
        x2 = x[rs] + _rms(out, gpost_ref[...])
        o_ref[:, pl.ds(hf * th_, th_), :] = jnp.swapaxes(x2.reshape(th_, nsb, D), 0, 1)
        if with_q:
            q = _dot(_rms(x2, gq_ref[...]).astype(BF), wq_ref[...])
            q_ref[:, pl.ds(hf * th_, th_), :] = jnp.swapaxes(q.reshape(th_, nsb, D), 0, 1)
    if fused:
        ffn_out(xn_ref, (j + 2) % N_X1)

    @pl.when(j == pl.num_programs(1) - 1)
    def _():
        nc4_ref[...] = zb[pl.ds(0, hist4)].reshape(hist4, nsb, D)
        nh_ref[...] = h_carry[...].reshape(nsb, D)
        for lc in range(n_lc):
            nc31_ref[:, :, pl.ds(lc * LANE, LANE)] = cb[lc, pl.ds(0, hist31)].reshape(
                hist31, nsb, LANE)


def _mixer_call(x, hist4, h0, hist31, mix, q_params, ffn_params, nsb, tt):
    fused = ffn_params is not None
    if fused:
        ns, t, _ = x.shape
    else:
        t, ns, _ = x.shape
    n_t = t // tt
    with_q = q_params is not None
    vec = _const_spec((1, D))
    seq_major = pl.BlockSpec((nsb, tt, D), lambda i, j: (i, j, 0))
    if fused:
        assert n_t >= 2 and (D // Z_BLOCK) % 2 == 0
        in_specs = [pl.BlockSpec((nsb, 2 * tt, D), lambda i, j: (i, 0, 0),
                                 pipeline_mode=pl.Buffered(1)),
                    pl.BlockSpec((nsb, tt, D), lambda i, j: (i, jnp.minimum(j + 2, n_t - 1), 0)),
                    vec, _const_spec((D_FF // FF_BLOCK, D, FF_BLOCK)),
                    _const_spec((D_FF // FF_BLOCK, D, FF_BLOCK)), _const_spec((D_FF, D)), vec]
        f_gpre, wg, wu, wd, f_gpost = ffn_params
        args = [x, x, f_gpre, _column_blocks(wg, FF_BLOCK), _column_blocks(wu, FF_BLOCK), wd,
                f_gpost]
    else:
        in_specs = [pl.BlockSpec((tt, nsb, D), lambda i, j: (j, i, 0))]
        args = [x]
    in_specs += [pl.BlockSpec((C4_W - 1, nsb, D), lambda i, j: (0, i, 0)),
                 pl.BlockSpec((nsb, D), lambda i, j: (i, 0)),
                 pl.BlockSpec((C31_W - 1, nsb, D), lambda i, j: (0, i, 0)),
                 vec, _const_spec((4 * D // Z_BLOCK, D, Z_BLOCK)), _const_spec((C4_W, D)), vec,
                 _const_spec((N_PAIR, PAIR_W, 2 * PAIR_W)), vec, vec, vec,
                 _const_spec((C31_W, D)), vec, vec, vec, _const_spec((2 * D, D)), vec]
    args += [hist4, h0, hist31, *mix]
    out_specs = [seq_major]
    out_shape = [jax.ShapeDtypeStruct((ns, t, D), F32)]
    if with_q:
        in_specs += [vec, _const_spec((D, D))]
        out_specs.append(seq_major)
        out_shape.append(jax.ShapeDtypeStruct((ns, t, D), F32))
        args += list(q_params)
    out_specs += [pl.BlockSpec((C4_W - 1, nsb, D), lambda i, j: (0, i, 0)),
                  pl.BlockSpec((nsb, D), lambda i, j: (i, 0)),
                  pl.BlockSpec((C31_W - 1, nsb, D), lambda i, j: (0, i, 0))]
    out_shape += [jax.ShapeDtypeStruct((C4_W - 1, ns, D), F32),
                  jax.ShapeDtypeStruct((ns, D), F32),
                  jax.ShapeDtypeStruct((C31_W - 1, ns, D), F32)]
    g = nsb // SUB
    rows = tt * nsb
    tile = lambda n: pltpu.VMEM((n, g, SUB, D), F32)
    col = lambda n: pltpu.VMEM((D // LANE, n, g, SUB, LANE), F32)
    scratch = [pltpu.VMEM((4 * D // Z_BLOCK, rows, Z_BLOCK), F32),
               tile(C4_W - 1 + tt), col(C31_W - 1 + tt), tile(tt), tile(tt), col(tt),
               pltpu.VMEM((rows, D), F32), pltpu.VMEM((g, SUB, D), F32),
               pltpu.VMEM((C4_W, SUB, D), F32),
               pltpu.VMEM((D // LANE, C31_W + 1, SUB, LANE), F32)]
    if fused:
        scratch += [pltpu.VMEM((N_X1, rows, D), F32), pltpu.VMEM((rows, D), BF),
                    pltpu.VMEM((rows, D), BF), pltpu.VMEM((D_FF // FF_BLOCK, rows, FF_BLOCK), BF)]
    return pl.pallas_call(
        functools.partial(_mixer_kernel, nsb=nsb, tt=tt, with_q=with_q, fused=fused),
        grid=(ns // nsb, n_t),
        in_specs=in_specs, out_specs=out_specs, out_shape=out_shape,
        scratch_shapes=scratch,
        compiler_params=_params(2),
        name="ffn1_mixer" if fused else "mixer",
    )(*args)


def _memkv_kernel(m_ref, g_ref, wk_ref, wv_ref, k5_ref, v5_ref, kb_ref, vb_ref, *, nb):
    m = _rms(m_ref[...], g_ref[...]).astype(BF)
    k = _dot(m, wk_ref[...])
    v = _dot(m, wv_ref[...])
    k5_ref[...] = k.reshape(nb, N_MEM, N_HEADS, HEAD_D)
    v5_ref[...] = v.reshape(nb, N_MEM, N_HEADS, HEAD_D)
    kb_ref[...] = k.astype(BF)
    vb_ref[...] = v.astype(BF)


def _memkv_call(mem2d, g, wk, wv, nb):
    rows = mem2d.shape[0]
    b = rows // N_MEM
    tm = nb * N_MEM
    kv5 = pl.BlockSpec((nb, N_MEM, N_HEADS, HEAD_D), lambda i: (i, 0, 0, 0))
    row = pl.BlockSpec((tm, D), lambda i: (i, 0))
    return pl.pallas_call(
        functools.partial(_memkv_kernel, nb=nb),
        grid=(b // nb,),
        in_specs=[row, _const_spec((1, D)), _const_spec((D, D)), _const_spec((D, D))],
        out_specs=[kv5, kv5, row, row],
        out_shape=[jax.ShapeDtypeStruct((b, N_MEM, N_HEADS, HEAD_D), F32),
                   jax.ShapeDtypeStruct((b, N_MEM, N_HEADS, HEAD_D), F32),
                   jax.ShapeDtypeStruct((rows, D), BF), jax.ShapeDtypeStruct((rows, D), BF)],
        compiler_params=_params(1),
        name="memkv",
    )(mem2d, g, wk, wv)


def _attn_ffn_prompt_kernel(x_ref, k_ref, v_ref, gq_ref, wq_ref, wo_ref, gao_ref,
                            gpre_ref, wg_ref, wu_ref, wd_ref, gpost_ref, o_ref):
    x = x_ref[...]
    q = _dot(_rms(x, gq_ref[...]).astype(BF), wq_ref[...])
    k = k_ref[...]
    v = v_ref[...]
    outs = []
    for hh in range(N_HEADS):
        sl = slice(hh * HEAD_D, (hh + 1) * HEAD_D)
        s = lax.dot_general(q[:, sl].astype(BF), k[:, sl], (((1,), (1,)), ((), ())),
                            preferred_element_type=F32) * (HEAD_D ** -0.5)
        e = jnp.exp(s - jnp.max(s, axis=-1, keepdims=True))
        p = e / jnp.sum(e, axis=-1, keepdims=True)
        outs.append(_dot(p.astype(BF), v[:, sl]))
    o = jnp.concatenate(outs, axis=-1)
    x3 = x + _rms(_dot(o.astype(BF), wo_ref[...]), gao_ref[...])
    o_ref[...] = _ffn(x3, gpre_ref[...], wg_ref, wu_ref, wd_ref, gpost_ref[...])


def _attn_ffn_prompt_call(x3d, k3d, v3d, gq, wq, wo, gao, gpre, wg, wu, wd, gpost, tm):
    b, s, _ = x3d.shape
    vec = _const_spec((1, D))
    return pl.pallas_call(
        _attn_ffn_prompt_kernel,
        grid=(b, s // tm),
        in_specs=[pl.BlockSpec((None, tm, D), lambda i, j: (i, j, 0)),
                  pl.BlockSpec((None, N_MEM, D), lambda i, j: (i, 0, 0)),
                  pl.BlockSpec((None, N_MEM, D), lambda i, j: (i, 0, 0)),
                  vec, _const_spec((D, D)), _const_spec((D, D)), vec,
                  vec, _const_spec((D, D_FF)), _const_spec((D, D_FF)), _const_spec((D_FF, D)), vec],
        out_specs=pl.BlockSpec((None, tm, D), lambda i, j: (i, j, 0)),
        out_shape=jax.ShapeDtypeStruct((b, s, D), F32),
        compiler_params=_params(2),
        name="attn_ffn_prompt",
    )(x3d, k3d, v3d, gq, wq, wo, gao, gpre, wg, wu, wd, gpost)


def _attn_sample_kernel(q_ref, k_ref, v_ref, o_ref, *, sq):
    rows = N_HEADS * SUB
    cols = N_MEM * N_HEADS
    own_head = (lax.broadcasted_iota(jnp.int32, (rows, cols), 1) % N_HEADS
                == lax.broadcasted_iota(jnp.int32, (rows, cols), 0) // SUB)
    for s in range(sq):
        kn = k_ref[s].reshape(cols, HEAD_D).astype(BF)
        vn = v_ref[s].reshape(cols, HEAD_D).astype(BF)
        q = q_ref[pl.ds(s * SUB, SUB), :]
        qh = jnp.concatenate([q[:, h * HEAD_D:(h + 1) * HEAD_D] for h in range(N_HEADS)], axis=0)
        sc = lax.dot_general(qh.astype(BF), kn, (((1,), (1,)), ((), ())),
                             preferred_element_type=F32) * (HEAD_D ** -0.5)
        sc = jnp.where(own_head, sc, -1e30)
        e = jnp.exp(sc - jnp.max(sc, axis=-1, keepdims=True))
        p = e / jnp.sum(e, axis=-1, keepdims=True)
        oh = _dot(p.astype(BF), vn)
        for h in range(N_HEADS):
            o_ref[pl.ds(s * SUB, SUB), pl.ds(h * HEAD_D, HEAD_D)] = oh[h * SUB:(h + 1) * SUB, :]


def _attn_sample_call(q2d, k5, v5, sq):
    n_seq = k5.shape[1]
    rows = sq * SUB
    kv = pl.BlockSpec((None, sq, N_MEM, N_HEADS, HEAD_D), lambda i: (0, i, 0, 0, 0))
    return pl.pallas_call(
        functools.partial(_attn_sample_kernel, sq=sq),
        grid=(n_seq // sq,),
        in_specs=[pl.BlockSpec((rows, D), lambda i: (i, 0)), kv, kv],
        out_specs=pl.BlockSpec((rows, D), lambda i: (i, 0)),
        out_shape=jax.ShapeDtypeStruct((n_seq * SUB, D), F32),
        compiler_params=_params(1),
        name="attn_sample",
    )(q2d, k5, v5)


def _oproj_ffn_kernel(x_ref, a_ref, wo_ref, gao_ref, gpre_ref, wg_ref, wu_ref, wd_ref, gpost_ref,
                      o_ref):
    x3 = x_ref[...] + _rms(_dot(a_ref[...].astype(BF), wo_ref[...]), gao_ref[...])
    o_ref[...] = _ffn(x3, gpre_ref[...], wg_ref, wu_ref, wd_ref, gpost_ref[...])


def _oproj_ffn_call(x2d, a2d, wo, gao, gpre, wg, wu, wd, gpost, tm):
    rows = x2d.shape[0]
    vec = _const_spec((1, D))
    row = pl.BlockSpec((tm, D), lambda i: (i, 0))
    return pl.pallas_call(
        _oproj_ffn_kernel,
        grid=(rows // tm,),
        in_specs=[row, row, _const_spec((D, D)), vec,
                  vec, _const_spec((D, D_FF)), _const_spec((D, D_FF)), _const_spec((D_FF, D)), vec],
        out_specs=row,
        out_shape=jax.ShapeDtypeStruct((rows, D), F32),
        compiler_params=_params(1),
        name="oproj_ffn",
    )(x2d, a2d, wo, gao, gpre, wg, wu, wd, gpost)


def _pair_gate_weights(w_a, w_x):
    hd = w_a.shape[-1]

    def blockdiag(w):
        w = w.reshape(N_PAIR, 2, hd, hd)
        z = jnp.zeros((N_PAIR, hd, hd), w.dtype)
        top = jnp.concatenate([w[:, 0], z], axis=-1)
        bot = jnp.concatenate([z, w[:, 1]], axis=-1)
        return jnp.concatenate([top, bot], axis=-2)

    return jnp.concatenate([blockdiag(w_a), blockdiag(w_x)], axis=-1).astype(BF)


def _column_blocks(w, block):
    k, n = w.shape
    return jnp.swapaxes(w.reshape(k, n // block, block), 0, 1)


def _seq_first(a):
    return jnp.swapaxes(a, 0, 1)


def _layer(l, xp, xs, mem_prompt, state_lru_conv, state_lru_h, state_conf_conv, cache_mem_k,
           cache_mem_v, w):
    bp, sp, _ = xp.shape
    bs, ss, _ = xs.shape
    vec = lambda name: w[name][l].reshape(1, D)
    bf = lambda name: w[name][l].astype(BF)

    ff1 = (vec('g_ff1_pre'), bf('ff1_w_gate'), bf('ff1_w_up'), bf('ff1_w_down'), vec('g_ff1_post'))
    ff2 = (vec('g_ff2_pre'), bf('ff2_w_gate'), bf('ff2_w_up'), bf('ff2_w_down'), vec('g_ff2_post'))
    mix = (vec('g_mix_pre'), _column_blocks(bf('w_in'), Z_BLOCK), w['lru_conv_w'][l],
           vec('lru_conv_b'),
           _pair_gate_weights(w['lru_w_a'][l], w['lru_w_x'][l]), vec('lru_b_a'), vec('lru_b_x'),
           vec('lru_lambda'), w['conf_conv_w'][l], vec('conf_conv_b'), vec('conf_ln_g'),
           vec('conf_ln_b'), bf('w_out'), vec('g_mix_post'))
    wq, wo = bf('w_q'), bf('w_o')

    mk5, mv5, mkb, mvb = _memkv_call(mem_prompt.reshape(bp * N_MEM, D), vec('g_mem_kv'),
                                     bf('w_mem_k'), bf('w_mem_v'), nb=2)
    x2, c4p, hp, c31p = _mixer_call(
        xp, jnp.zeros((C4_W - 1, bp, D), F32), jnp.zeros((bp, D), F32),
        jnp.zeros((C31_W - 1, bp, D), F32), mix, None, ff1, nsb=bp, tt=32)
    xp_out = _attn_ffn_prompt_call(x2, mkb.reshape(bp, N_MEM, D), mvb.reshape(bp, N_MEM, D),
                                   vec('g_mem_pre'), wq, wo, vec('g_mem_post'), *ff2, tm=512)

    s1 = _ffn_tmajor_call(xs, *ff1, nsb=64, tt=ss)
    s2, q, c4s, hs, c31s = _mixer_call(
        s1, _seq_first(state_lru_conv[l]), state_lru_h[l], _seq_first(state_conf_conv[l]), mix,
        (vec('g_mem_pre'), wq), None, nsb=32, tt=ss)
    att = _attn_sample_call(q.reshape(bs * ss, D), cache_mem_k[l:l + 1], cache_mem_v[l:l + 1], sq=4)
    xs_out = _oproj_ffn_call(s2.reshape(bs * ss, D), att, wo, vec('g_mem_post'), *ff2,
                             tm=512).reshape(bs, ss, D)

    return (xp_out, xs_out, _seq_first(c4p), hp, _seq_first(c31p), mk5, mv5,
            _seq_first(c4s), hs, _seq_first(c31s))


def kernel(x_prompt, x_sample, mem_prompt, state_lru_conv, state_lru_h, state_conf_conv, cache_mem_k, cache_mem_v, g_ff1_pre, ff1_w_gate, ff1_w_up, ff1_w_down, g_ff1_post, g_mix_pre, w_in, lru_conv_w, lru_conv_b, lru_w_a, lru_b_a, lru_w_x, lru_b_x, lru_lambda, conf_conv_w, conf_conv_b, conf_ln_g, conf_ln_b, w_out, g_mix_post, g_mem_pre, g_mem_kv, w_mem_k, w_mem_v, w_q, w_o, g_mem_post, g_ff2_pre, ff2_w_gate, ff2_w_up, ff2_w_down, g_ff2_post):
    w = dict(g_ff1_pre=g_ff1_pre, ff1_w_gate=ff1_w_gate, ff1_w_up=ff1_w_up, ff1_w_down=ff1_w_down,
             g_ff1_post=g_ff1_post, g_mix_pre=g_mix_pre, w_in=w_in, lru_conv_w=lru_conv_w,
             lru_conv_b=lru_conv_b, lru_w_a=lru_w_a, lru_b_a=lru_b_a, lru_w_x=lru_w_x,
             lru_b_x=lru_b_x, lru_lambda=lru_lambda, conf_conv_w=conf_conv_w,
             conf_conv_b=conf_conv_b, conf_ln_g=conf_ln_g, conf_ln_b=conf_ln_b, w_out=w_out,
             g_mix_post=g_mix_post, g_mem_pre=g_mem_pre, g_mem_kv=g_mem_kv, w_mem_k=w_mem_k,
             w_mem_v=w_mem_v, w_q=w_q, w_o=w_o, g_mem_post=g_mem_post, g_ff2_pre=g_ff2_pre,
             ff2_w_gate=ff2_w_gate, ff2_w_up=ff2_w_up, ff2_w_down=ff2_w_down, g_ff2_post=g_ff2_post)
    depth = w_in.shape[0]
    xp, xs = x_prompt, x_sample
    per_layer = []
    for l in range(depth):
        outs = _layer(l, xp, xs, mem_prompt, state_lru_conv, state_lru_h, state_conf_conv,
                      cache_mem_k, cache_mem_v, w)
        xp, xs = outs[0], outs[1]
        per_layer.append(outs[2:])
    stacked = [jnp.stack([pl_[i] for pl_ in per_layer]) for i in range(8)]
    return (xp, xs, *stacked)
```

```python
import functools

import jax
import jax.numpy as jnp
from jax import lax
from jax.experimental import pallas as pl
from jax.experimental.pallas import tpu as pltpu

D = 1024
D_FF = 2816
N_PAIR = 4
PAIR_W = 256
LRU_C = 8.0
C4_W = 4
C31_W = 31
N_MEM = 256
N_HEADS = 4
HEAD_D = 256
FFN_RES = 0.5
EPS = 1e-6
SUB = 8
LANE = 128
CONV_CHUNK = 8
CONV_TAPS = 16
VMEM_LIMIT = 56 * 1024 * 1024

BF = jnp.bfloat16
F32 = jnp.float32


def _dot(a, b):
    return jnp.dot(a, b, preferred_element_type=F32)


def _rms(x, g):
    ms = jnp.mean(x * x, axis=-1, keepdims=True)
    return x * lax.rsqrt(ms + EPS) * g


def _sigmoid(x):
    return 1.0 / (1.0 + jnp.exp(-x))


def _gelu_tanh(x):
    return 0.5 * x * (1.0 + jnp.tanh(0.7978845608028654 * (x + 0.044715 * (x * x * x))))


def _ffn(x, gpre, wg_ref, wu_ref, wd_ref, gpost):
    h = _rms(x, gpre).astype(BF)
    gate = _dot(h, wg_ref[...])
    up = _dot(h, wu_ref[...])
    act = (gate * _sigmoid(gate) * up).astype(BF)
    y = _dot(act, wd_ref[...])
    return x + FFN_RES * _rms(y, gpost)


def _gate_math(r_pre, i_pre, lam, xc):
    r = _sigmoid(r_pre)
    i = _sigmoid(i_pre)
    softplus_neg_lam = jnp.maximum(-lam, 0.0) + jnp.log1p(jnp.exp(-jnp.abs(lam)))
    log_a = (-LRU_C * r) * softplus_neg_lam
    a = jnp.exp(log_a)
    th = jnp.tanh(log_a)
    y = (-2.0 * th) / (1.0 - th)
    root = jnp.where(y == 0.0, 0.0, y * lax.rsqrt(y))
    return a, root * (i * xc)


def _layernorm(x, g, b):
    xc = x - jnp.mean(x, axis=-1, keepdims=True)
    return xc * lax.rsqrt(jnp.mean(xc * xc, axis=-1, keepdims=True) + EPS) * g + b


def _const_spec(shape):
    n = len(shape)
    return pl.BlockSpec(shape, lambda *_: (0,) * n, pipeline_mode=pl.Buffered(1))


def _params(n_grid):
    return pltpu.CompilerParams(dimension_semantics=("arbitrary",) * n_grid,
                                vmem_limit_bytes=VMEM_LIMIT)


def _ffn_tmajor_kernel(x_ref, gpre_ref, wg_ref, wu_ref, wd_ref, gpost_ref, o_ref, *, nsb, tt):
    hs = nsb // 2
    for i in range(2):
        x = x_ref[pl.ds(i * hs, hs)].reshape(hs * tt, D)
        y = _ffn(x, gpre_ref[...], wg_ref, wu_ref, wd_ref, gpost_ref[...])
        o_ref[:, pl.ds(i * hs, hs), :] = jnp.swapaxes(y.reshape(hs, tt, D), 0, 1)


def _ffn_tmajor_call(x3d, gpre, wg, wu, wd, gpost, nsb, tt):
    ns, t, _ = x3d.shape
    return pl.pallas_call(
        functools.partial(_ffn_tmajor_kernel, nsb=nsb, tt=tt),
        grid=(ns // nsb, t // tt),
        in_specs=[pl.BlockSpec((nsb, tt, D), lambda i, j: (i, j, 0)),
                  _const_spec((1, D)), _const_spec((D, D_FF)), _const_spec((D, D_FF)),
                  _const_spec((D_FF, D)), _const_spec((1, D))],
        out_specs=pl.BlockSpec((tt, nsb, D), lambda i, j: (j, i, 0)),
        out_shape=jax.ShapeDtypeStruct((t, ns, D), F32),
        compiler_params=_params(2),
        name="ffn1",
    )(x3d, gpre, wg, wu, wd, gpost)


def _mixer_kernel(*refs, nsb, tt, with_q):
    (x_ref, h4_ref, h0_ref, h31_ref, gpre_ref, win_ref, c4w_ref, c4b_ref, wax_ref, ba_ref, bx_ref,
     lam_ref, c31w_ref, c31b_ref, lng_ref, lnb_ref, wout_ref, gpost_ref) = refs[:18]
    refs = refs[18:]
    if with_q:
        gq_ref, wq_ref, o_ref, q_ref = refs[:4]
        refs = refs[4:]
    else:
        o_ref = refs[0]
        refs = refs[1:]
    nc4_ref, nh_ref, nc31_ref, zb, cb, a_buf, u_buf, c_buf, gelu_buf, h_carry, w4b, w31b = refs

    g = nsb // SUB
    rows = tt * nsb
    hist4, hist31 = C4_W - 1, C31_W - 1
    n_lc = D // LANE
    j = pl.program_id(1)
    last = j == pl.num_programs(1) - 1

    @pl.when((pl.program_id(0) == 0) & (j == 0))
    def _():
        for k in range(C4_W):
            w4b[k] = jnp.broadcast_to(c4w_ref[pl.ds(k, 1), :], (SUB, D))
        for k in range(C31_W):
            for lc in range(n_lc):
                w31b[lc, k] = jnp.broadcast_to(c31w_ref[pl.ds(k, 1), pl.ds(lc * LANE, LANE)],
                                               (SUB, LANE))
        for lc in range(n_lc):
            w31b[lc, C31_W] = jnp.broadcast_to(c31b_ref[:, pl.ds(lc * LANE, LANE)], (SUB, LANE))

    @pl.when(j == 0)
    def _():
        zb[pl.ds(0, hist4)] = h4_ref[...].reshape(hist4, g, SUB, D)
        for lc in range(n_lc):
            cb[lc, pl.ds(0, hist31)] = h31_ref[:, :, pl.ds(lc * LANE, LANE)].reshape(
                hist31, g, SUB, LANE)
        h_carry[...] = h0_ref[...].reshape(g, SUB, D)

    x = x_ref[...].reshape(rows, D)
    h = _rms(x, gpre_ref[...]).astype(BF)

    zb[pl.ds(hist4, tt)] = _dot(h, win_ref[:, pl.ds(0, D)]).reshape(tt, g, SUB, D)
    xc = jnp.broadcast_to(c4b_ref[...].reshape(1, 1, 1, D), (tt, g, SUB, D))
    for k in range(C4_W):
        xc = xc + w4b[k][None, None] * zb[pl.ds(k, tt)]

    @pl.when(last)
    def _():
        nc4_ref[...] = zb[pl.ds(tt, hist4)].reshape(hist4, nsb, D)

    zb[pl.ds(0, hist4)] = zb[pl.ds(tt, hist4)]

    xcb = xc.reshape(rows, D).astype(BF)
    xc2 = xc.reshape(rows, D)
    for p in range(N_PAIR):
        sl = slice(p * PAIR_W, (p + 1) * PAIR_W)
        z_v = _dot(h, win_ref[:, pl.ds(2 * D + p * PAIR_W, PAIR_W)])
        z_gate = _dot(h, win_ref[:, pl.ds(3 * D + p * PAIR_W, PAIR_W)])
        g2 = _dot(xcb[:, sl], wax_ref[p])
        a, u = _gate_math(g2[:, :PAIR_W] + ba_ref[:, sl], g2[:, PAIR_W:] + bx_ref[:, sl],
                          lam_ref[:, sl], xc2[:, sl])
        a_buf[:, :, :, sl] = a.reshape(tt, g, SUB, PAIR_W)
        u_buf[:, :, :, sl] = u.reshape(tt, g, SUB, PAIR_W)
        gelu_buf[:, sl] = _gelu_tanh(_dot(h, win_ref[:, pl.ds(D + p * PAIR_W, PAIR_W)]))
        glu = (z_v * _sigmoid(z_gate)).reshape(tt, g, SUB, PAIR_W)
        for i_ in range(PAIR_W // LANE):
            cb[p * (PAIR_W // LANE) + i_, pl.ds(hist31, tt)] = glu[..., i_ * LANE:(i_ + 1) * LANE]

    def scan_body(t, hc):
        hc = a_buf[t] * hc + u_buf[t]
        u_buf[t] = hc
        return hc

    hc = lax.fori_loop(0, tt, scan_body, h_carry[...])
    h_carry[...] = hc

    @pl.when(last)
    def _():
        nh_ref[...] = hc.reshape(nsb, D)

    rb = min(tt, CONV_CHUNK)

    def conv_body(lc, carry):
        for k0 in range(0, C31_W, CONV_TAPS):
            nk = min(CONV_TAPS, C31_W - k0)
            wk = [w31b[lc, k0 + i] for i in range(nk)]
            for gi in range(g):
                for t0 in range(0, tt, rb):
                    if k0 == 0:
                        acc = [w31b[lc, C31_W] for _ in range(rb)]
                    else:
                        acc = [c_buf[lc, t0 + r, gi] for r in range(rb)]
                    for s in range(rb + nk - 1):
                        xs = cb[lc, t0 + k0 + s, gi]
                        for r in range(rb):
                            if 0 <= s - r < nk:
                                acc[r] = acc[r] + wk[s - r] * xs
                    for r in range(rb):
                        c_buf[lc, t0 + r, gi] = acc[r]
        return carry

    lax.fori_loop(0, n_lc, conv_body, 0)

    @pl.when(last)
    def _():
        for lc in range(n_lc):
            nc31_ref[:, :, pl.ds(lc * LANE, LANE)] = cb[lc, pl.ds(tt, hist31)].reshape(
                hist31, nsb, LANE)

    for lc in range(n_lc):
        for s in range(hist31):
            cb[lc, s] = cb[lc, tt + s]

    y_lru = (gelu_buf[...] * u_buf[...].reshape(rows, D)).astype(BF)
    out_lru = _dot(y_lru, wout_ref[pl.ds(0, D), :])
    c = jnp.concatenate([c_buf[lc].reshape(rows, LANE) for lc in range(n_lc)], axis=-1)
    yc = _layernorm(c, lng_ref[...], lnb_ref[...])
    y_conf = (yc * _sigmoid(yc)).astype(BF)
    th_ = tt // 2
    for hf in range(2):
        rs = slice(hf * th_ * nsb, (hf + 1) * th_ * nsb)
        out = out_lru[rs] + _dot(y_conf[rs], wout_ref[pl.ds(D, D), :])
        x2 = x[rs] + _rms(out, gpost_ref[...])
        o_ref[:, pl.ds(hf * th_, th_), :] = jnp.swapaxes(x2.reshape(th_, nsb, D), 0, 1)
        if with_q:
            q = _dot(_rms(x2, gq_ref[...]).astype(BF), wq_ref[...])
            q_ref[:, pl.ds(hf * th_, th_), :] = jnp.swapaxes(q.reshape(th_, nsb, D), 0, 1)


def _mixer_call(x_tm, hist4, h0, hist31, mix, q_params, nsb, tt):
    t, ns, _ = x_tm.shape
    with_q = q_params is not None
    vec = _const_spec((1, D))
    seq_major = pl.BlockSpec((nsb, tt, D), lambda i, j: (i, j, 0))
    in_specs = [pl.BlockSpec((tt, nsb, D), lambda i, j: (j, i, 0)),
                pl.BlockSpec((C4_W - 1, nsb, D), lambda i, j: (0, i, 0)),
                pl.BlockSpec((nsb, D), lambda i, j: (i, 0)),
                pl.BlockSpec((C31_W - 1, nsb, D), lambda i, j: (0, i, 0)),
                vec, _const_spec((D, 4 * D)), _const_spec((C4_W, D)), vec,
                _const_spec((N_PAIR, PAIR_W, 2 * PAIR_W)), vec, vec, vec,
                _const_spec((C31_W, D)), vec, vec, vec, _const_spec((2 * D, D)), vec]
    out_specs = [seq_major]
    out_shape = [jax.ShapeDtypeStruct((ns, t, D), F32)]
    args = [x_tm, hist4, h0, hist31, *mix]
    if with_q:
        in_specs += [vec, _const_spec((D, D))]
        out_specs.append(seq_major)
        out_shape.append(jax.ShapeDtypeStruct((ns, t, D), F32))
        args += list(q_params)
    out_specs += [pl.BlockSpec((C4_W - 1, nsb, D), lambda i, j: (0, i, 0)),
                  pl.BlockSpec((nsb, D), lambda i, j: (i, 0)),
                  pl.BlockSpec((C31_W - 1, nsb, D), lambda i, j: (0, i, 0))]
    out_shape += [jax.ShapeDtypeStruct((C4_W - 1, ns, D), F32),
                  jax.ShapeDtypeStruct((ns, D), F32),
                  jax.ShapeDtypeStruct((C31_W - 1, ns, D), F32)]
    g = nsb // SUB
    tile = lambda n: pltpu.VMEM((n, g, SUB, D), F32)
    col = lambda n: pltpu.VMEM((D // LANE, n, g, SUB, LANE), F32)
    return pl.pallas_call(
        functools.partial(_mixer_kernel, nsb=nsb, tt=tt, with_q=with_q),
        grid=(ns // nsb, t // tt),
        in_specs=in_specs, out_specs=out_specs, out_shape=out_shape,
        scratch_shapes=[tile(C4_W - 1 + tt), col(C31_W - 1 + tt), tile(tt), tile(tt), col(tt),
                        pltpu.VMEM((tt * nsb, D), F32),
                        pltpu.VMEM((g, SUB, D), F32),
                        pltpu.VMEM((C4_W, SUB, D), F32),
                        pltpu.VMEM((D // LANE, C31_W + 1, SUB, LANE), F32)],
        compiler_params=_params(2),
        name="mixer",
    )(*args)


def _memkv_kernel(m_ref, g_ref, wk_ref, wv_ref, k5_ref, v5_ref, kb_ref, vb_ref, *, nb):
    m = _rms(m_ref[...], g_ref[...]).astype(BF)
    k = _dot(m, wk_ref[...])
    v = _dot(m, wv_ref[...])
    k5_ref[...] = k.reshape(nb, N_MEM, N_HEADS, HEAD_D)
    v5_ref[...] = v.reshape(nb, N_MEM, N_HEADS, HEAD_D)
    kb_ref[...] = k.astype(BF)
    vb_ref[...] = v.astype(BF)


def _memkv_call(mem2d, g, wk, wv, nb):
    rows = mem2d.shape[0]
    b = rows // N_MEM
    tm = nb * N_MEM
    kv5 = pl.BlockSpec((nb, N_MEM, N_HEADS, HEAD_D), lambda i: (i, 0, 0, 0))
    row = pl.BlockSpec((tm, D), lambda i: (i, 0))
    return pl.pallas_call(
        functools.partial(_memkv_kernel, nb=nb),
        grid=(b // nb,),
        in_specs=[row, _const_spec((1, D)), _const_spec((D, D)), _const_spec((D, D))],
        out_specs=[kv5, kv5, row, row],
        out_shape=[jax.ShapeDtypeStruct((b, N_MEM, N_HEADS, HEAD_D), F32),
                   jax.ShapeDtypeStruct((b, N_MEM, N_HEADS, HEAD_D), F32),
                   jax.ShapeDtypeStruct((rows, D), BF), jax.ShapeDtypeStruct((rows, D), BF)],
        compiler_params=_params(1),
        name="memkv",
    )(mem2d, g, wk, wv)


def _attn_ffn_prompt_kernel(x_ref, k_ref, v_ref, gq_ref, wq_ref, wo_ref, gao_ref,
                            gpre_ref, wg_ref, wu_ref, wd_ref, gpost_ref, o_ref):
    x = x_ref[...]
    q = _dot(_rms(x, gq_ref[...]).astype(BF), wq_ref[...])
    k = k_ref[...]
    v = v_ref[...]
    outs = []
    for hh in range(N_HEADS):
        sl = slice(hh * HEAD_D, (hh + 1) * HEAD_D)
        s = lax.dot_general(q[:, sl].astype(BF), k[:, sl], (((1,), (1,)), ((), ())),
                            preferred_element_type=F32) * (HEAD_D ** -0.5)
        e = jnp.exp(s - jnp.max(s, axis=-1, keepdims=True))
        p = e / jnp.sum(e, axis=-1, keepdims=True)
        outs.append(_dot(p.astype(BF), v[:, sl]))
    o = jnp.concatenate(outs, axis=-1)
    x3 = x + _rms(_dot(o.astype(BF), wo_ref[...]), gao_ref[...])
    o_ref[...] = _ffn(x3, gpre_ref[...], wg_ref, wu_ref, wd_ref, gpost_ref[...])


def _attn_ffn_prompt_call(x3d, k3d, v3d, gq, wq, wo, gao, gpre, wg, wu, wd, gpost, tm):
    b, s, _ = x3d.shape
    vec = _const_spec((1, D))
    return pl.pallas_call(
        _attn_ffn_prompt_kernel,
        grid=(b, s // tm),
        in_specs=[pl.BlockSpec((None, tm, D), lambda i, j: (i, j, 0)),
                  pl.BlockSpec((None, N_MEM, D), lambda i, j: (i, 0, 0)),
                  pl.BlockSpec((None, N_MEM, D), lambda i, j: (i, 0, 0)),
                  vec, _const_spec((D, D)), _const_spec((D, D)), vec,
                  vec, _const_spec((D, D_FF)), _const_spec((D, D_FF)), _const_spec((D_FF, D)), vec],
        out_specs=pl.BlockSpec((None, tm, D), lambda i, j: (i, j, 0)),
        out_shape=jax.ShapeDtypeStruct((b, s, D), F32),
        compiler_params=_params(2),
        name="attn_ffn_prompt",
    )(x3d, k3d, v3d, gq, wq, wo, gao, gpre, wg, wu, wd, gpost)


def _attn_sample_kernel(q_ref, k_ref, v_ref, o_ref, *, sq):
    rows = N_HEADS * SUB
    cols = N_MEM * N_HEADS
    own_head = (lax.broadcasted_iota(jnp.int32, (rows, cols), 1) % N_HEADS
                == lax.broadcasted_iota(jnp.int32, (rows, cols), 0) // SUB)
    scores = []
    for s in range(sq):
        kn = k_ref[s].reshape(cols, HEAD_D).astype(BF)
        q = q_ref[pl.ds(s * SUB, SUB), :]
        qh = jnp.concatenate([q[:, h * HEAD_D:(h + 1) * HEAD_D] for h in range(N_HEADS)], axis=0)
        scores.append(lax.dot_general(qh.astype(BF), kn, (((1,), (1,)), ((), ())),
                                      preferred_element_type=F32) * (HEAD_D ** -0.5))
    probs = []
    for s in range(sq):
        sc = jnp.where(own_head, scores[s], -1e30)
        e = jnp.exp(sc - jnp.max(sc, axis=-1, keepdims=True))
        probs.append((e / jnp.sum(e, axis=-1, keepdims=True)).astype(BF))
    for s in range(sq):
        vn = v_ref[s].reshape(cols, HEAD_D).astype(BF)
        oh = _dot(probs[s], vn)
        for h in range(N_HEADS):
            o_ref[pl.ds(s * SUB, SUB), pl.ds(h * HEAD_D, HEAD_D)] = oh[h * SUB:(h + 1) * SUB, :]


def _attn_sample_call(q2d, k5, v5, sq):
    n_seq = k5.shape[1]
    rows = sq * SUB
    kv = pl.BlockSpec((None, sq, N_MEM, N_HEADS, HEAD_D), lambda i: (0, i, 0, 0, 0))
    return pl.pallas_call(
        functools.partial(_attn_sample_kernel, sq=sq),
        grid=(n_seq // sq,),
        in_specs=[pl.BlockSpec((rows, D), lambda i: (i, 0)), kv, kv],
        out_specs=pl.BlockSpec((rows, D), lambda i: (i, 0)),
        out_shape=jax.ShapeDtypeStruct((n_seq * SUB, D), F32),
        compiler_params=_params(1),
        name="attn_sample",
    )(q2d, k5, v5)


def _oproj_ffn_kernel(x_ref, a_ref, wo_ref, gao_ref, gpre_ref, wg_ref, wu_ref, wd_ref, gpost_ref,
                      o_ref):
    x3 = x_ref[...] + _rms(_dot(a_ref[...].astype(BF), wo_ref[...]), gao_ref[...])
    o_ref[...] = _ffn(x3, gpre_ref[...], wg_ref, wu_ref, wd_ref, gpost_ref[...])


def _oproj_ffn_call(x2d, a2d, wo, gao, gpre, wg, wu, wd, gpost, tm):
    rows = x2d.shape[0]
    vec = _const_spec((1, D))
    row = pl.BlockSpec((tm, D), lambda i: (i, 0))
    return pl.pallas_call(
        _oproj_ffn_kernel,
        grid=(rows // tm,),
        in_specs=[row, row, _const_spec((D, D)), vec,
                  vec, _const_spec((D, D_FF)), _const_spec((D, D_FF)), _const_spec((D_FF, D)), vec],
        out_specs=row,
        out_shape=jax.ShapeDtypeStruct((rows, D), F32),
        compiler_params=_params(1),
        name="oproj_ffn",
    )(x2d, a2d, wo, gao, gpre, wg, wu, wd, gpost)


def _pair_gate_weights(w_a, w_x):
    hd = w_a.shape[-1]

    def blockdiag(w):
        w = w.reshape(N_PAIR, 2, hd, hd)
        z = jnp.zeros((N_PAIR, hd, hd), w.dtype)
        top = jnp.concatenate([w[:, 0], z], axis=-1)
        bot = jnp.concatenate([z, w[:, 1]], axis=-1)
        return jnp.concatenate([top, bot], axis=-2)

    return jnp.concatenate([blockdiag(w_a), blockdiag(w_x)], axis=-1).astype(BF)


def _seq_first(a):
    return jnp.swapaxes(a, 0, 1)


def _layer(l, xp, xs, mem_prompt, state_lru_conv, state_lru_h, state_conf_conv, cache_mem_k,
           cache_mem_v, w):
    bp, sp, _ = xp.shape
    bs, ss, _ = xs.shape
    vec = lambda name: w[name][l].reshape(1, D)
    bf = lambda name: w[name][l].astype(BF)

    ff1 = (vec('g_ff1_pre'), bf('ff1_w_gate'), bf('ff1_w_up'), bf('ff1_w_down'), vec('g_ff1_post'))
    ff2 = (vec('g_ff2_pre'), bf('ff2_w_gate'), bf('ff2_w_up'), bf('ff2_w_down'), vec('g_ff2_post'))
    mix = (vec('g_mix_pre'), bf('w_in'), w['lru_conv_w'][l], vec('lru_conv_b'),
           _pair_gate_weights(w['lru_w_a'][l], w['lru_w_x'][l]), vec('lru_b_a'), vec('lru_b_x'),
           vec('lru_lambda'), w['conf_conv_w'][l], vec('conf_conv_b'), vec('conf_ln_g'),
           vec('conf_ln_b'), bf('w_out'), vec('g_mix_post'))
    wq, wo = bf('w_q'), bf('w_o')

    mk5, mv5, mkb, mvb = _memkv_call(mem_prompt.reshape(bp * N_MEM, D), vec('g_mem_kv'),
                                     bf('w_mem_k'), bf('w_mem_v'), nb=2)
    x1 = _ffn_tmajor_call(xp, *ff1, nsb=bp, tt=64)
    x2, c4p, hp, c31p = _mixer_call(
        x1, jnp.zeros((C4_W - 1, bp, D), F32), jnp.zeros((bp, D), F32),
        jnp.zeros((C31_W - 1, bp, D), F32), mix, None, nsb=bp, tt=64)
    xp_out = _attn_ffn_prompt_call(x2, mkb.reshape(bp, N_MEM, D), mvb.reshape(bp, N_MEM, D),
                                   vec('g_mem_pre'), wq, wo, vec('g_mem_post'), *ff2, tm=512)

    s1 = _ffn_tmajor_call(xs, *ff1, nsb=64, tt=ss)
    s2, q, c4s, hs, c31s = _mixer_call(
        s1, _seq_first(state_lru_conv[l]), state_lru_h[l], _seq_first(state_conf_conv[l]), mix,
        (vec('g_mem_pre'), wq), nsb=32, tt=ss)
    att = _attn_sample_call(q.reshape(bs * ss, D), cache_mem_k[l:l + 1], cache_mem_v[l:l + 1], sq=8)
    xs_out = _oproj_ffn_call(s2.reshape(bs * ss, D), att, wo, vec('g_mem_post'), *ff2,
                             tm=512).reshape(bs, ss, D)

    return (xp_out, xs_out, _seq_first(c4p), hp, _seq_first(c31p), mk5, mv5,
            _seq_first(c4s), hs, _seq_first(c31s))


def kernel(x_prompt, x_sample, mem_prompt, state_lru_conv, state_lru_h, state_conf_conv, cache_mem_k, cache_mem_v, g_ff1_pre, ff1_w_gate, ff1_w_up, ff1_w_down, g_ff1_post, g_mix_pre, w_in, lru_conv_w, lru_conv_b, lru_w_a, lru_b_a, lru_w_x, lru_b_x, lru_lambda, conf_conv_w, conf_conv_b, conf_ln_g, conf_ln_b, w_out, g_mix_post, g_mem_pre, g_mem_kv, w_mem_k, w_mem_v, w_q, w_o, g_mem_post, g_ff2_pre, ff2_w_gate, ff2_w_up, ff2_w_down, g_ff2_post):
    w = dict(g_ff1_pre=g_ff1_pre, ff1_w_gate=ff1_w_gate, ff1_w_up=ff1_w_up, ff1_w_down=ff1_w_down,
             g_ff1_post=g_ff1_post, g_mix_pre=g_mix_pre, w_in=w_in, lru_conv_w=lru_conv_w,
             lru_conv_b=lru_conv_b, lru_w_a=lru_w_a, lru_b_a=lru_b_a, lru_w_x=lru_w_x,
             lru_b_x=lru_b_x, lru_lambda=lru_lambda, conf_conv_w=conf_conv_w,
             conf_conv_b=conf_conv_b, conf_ln_g=conf_ln_g, conf_ln_b=conf_ln_b, w_out=w_out,
             g_mix_post=g_mix_post, g_mem_pre=g_mem_pre, g_mem_kv=g_mem_kv, w_mem_k=w_mem_k,
             w_mem_v=w_mem_v, w_q=w_q, w_o=w_o, g_mem_post=g_mem_post, g_ff2_pre=g_ff2_pre,
             ff2_w_gate=ff2_w_gate, ff2_w_up=ff2_w_up, ff2_w_down=ff2_w_down, g_ff2_post=g_ff2_post)
    depth = w_in.shape[0]
    xp, xs = x_prompt, x_sample
    per_layer = []
    for l in range(depth):
        outs = _layer(l, xp, xs, mem_prompt, state_lru_conv, state_lru_h, state_conf_conv,
                      cache_mem_k, cache_mem_v, w)
        xp, xs = outs[0], outs[1]
        per_layer.append(outs[2:])
    stacked = [jnp.stack([pl_[i] for pl_ in per_layer]) for i in range(8)]
    return (xp, xs, *stacked)
```

```python
import functools

import jax
import jax.numpy as jnp
from jax import lax
from jax.experimental import pallas as pl
from jax.experimental.pallas import tpu as pltpu

D = 1024
D_FF = 2816
N_PAIR = 4
PAIR_W = 256
LRU_C = 8.0
C4_W = 4
C31_W = 31
N_MEM = 256
N_HEADS = 4
HEAD_D = 256
FFN_RES = 0.5
EPS = 1e-6
SUB = 8
LANE = 128
CONV_CHUNK = 8
CONV_TAPS = 16
ROW_TILE = 512
PROMPT_TIME_TILE = 64
SAMPLE_MIXER_SEQS = 32
SAMPLE_ATTN_SEQS = 8
MEMKV_BATCHES = 2
VMEM_LIMIT = 56 * 1024 * 1024

BF = jnp.bfloat16
F32 = jnp.float32


def _dot(a, b):
    return jnp.dot(a, b, preferred_element_type=F32)


def _rms(x, g):
    ms = jnp.mean(x * x, axis=-1, keepdims=True)
    return x * lax.rsqrt(ms + EPS) * g


def _sigmoid(x):
    return 1.0 / (1.0 + jnp.exp(-x))


def _gelu_tanh(x):
    return 0.5 * x * (1.0 + jnp.tanh(0.7978845608028654 * (x + 0.044715 * (x * x * x))))


def _ffn(x, gpre, wg_ref, wu_ref, wd_ref, gpost):
    h = _rms(x, gpre).astype(BF)
    gate = _dot(h, wg_ref[...])
    up = _dot(h, wu_ref[...])
    act = (gate * _sigmoid(gate) * up).astype(BF)
    y = _dot(act, wd_ref[...])
    return x + FFN_RES * _rms(y, gpost)


def _gate_math(r_pre, i_pre, lam, xc):
    r = _sigmoid(r_pre)
    i = _sigmoid(i_pre)
    softplus_neg_lam = jnp.maximum(-lam, 0.0) + jnp.log1p(jnp.exp(-jnp.abs(lam)))
    log_a = (-LRU_C * r) * softplus_neg_lam
    a = jnp.exp(log_a)
    th = jnp.tanh(log_a)
    y = (-2.0 * th) / (1.0 - th)
    root = jnp.where(y == 0.0, 0.0, y * lax.rsqrt(y))
    return a, root * (i * xc)


def _layernorm(x, g, b):
    xc = x - jnp.mean(x, axis=-1, keepdims=True)
    return xc * lax.rsqrt(jnp.mean(xc * xc, axis=-1, keepdims=True) + EPS) * g + b


def _const_spec(shape):
    n = len(shape)
    return pl.BlockSpec(shape, lambda *_: (0,) * n, pipeline_mode=pl.Buffered(1))


def _params(n_grid):
    return pltpu.CompilerParams(dimension_semantics=("arbitrary",) * n_grid,
                                vmem_limit_bytes=VMEM_LIMIT)


def _ffn_tmajor_kernel(x_ref, gpre_ref, wg_ref, wu_ref, wd_ref, gpost_ref, o_ref, *, nsb, tt):
    hs = nsb // 2
    for i in range(2):
        x = x_ref[pl.ds(i * hs, hs)].reshape(hs * tt, D)
        y = _ffn(x, gpre_ref[...], wg_ref, wu_ref, wd_ref, gpost_ref[...])
        o_ref[:, pl.ds(i * hs, hs), :] = jnp.swapaxes(y.reshape(hs, tt, D), 0, 1)


def _ffn_tmajor_call(x3d, gpre, wg, wu, wd, gpost, nsb, tt):
    ns, t, _ = x3d.shape
    return pl.pallas_call(
        functools.partial(_ffn_tmajor_kernel, nsb=nsb, tt=tt),
        grid=(ns // nsb, t // tt),
        in_specs=[pl.BlockSpec((nsb, tt, D), lambda i, j: (i, j, 0)),
                  _const_spec((1, D)), _const_spec((D, D_FF)), _const_spec((D, D_FF)),
                  _const_spec((D_FF, D)), _const_spec((1, D))],
        out_specs=pl.BlockSpec((tt, nsb, D), lambda i, j: (j, i, 0)),
        out_shape=jax.ShapeDtypeStruct((t, ns, D), F32),
        compiler_params=_params(2),
        name="ffn1",
    )(x3d, gpre, wg, wu, wd, gpost)


def _mixer_kernel(*refs, nsb, tt, with_q):
    (x_ref, h4_ref, h0_ref, h31_ref, gpre_ref, win_ref, c4w_ref, c4b_ref, wax_ref, ba_ref, bx_ref,
     lam_ref, c31w_ref, c31b_ref, lng_ref, lnb_ref, wout_ref, gpost_ref) = refs[:18]
    refs = refs[18:]
    if with_q:
        gq_ref, wq_ref, o_ref, q_ref = refs[:4]
        refs = refs[4:]
    else:
        o_ref = refs[0]
        refs = refs[1:]
    nc4_ref, nh_ref, nc31_ref, zb, cb, a_buf, u_buf, c_buf, gelu_buf, h_carry, w4b, w31b = refs

    g = nsb // SUB
    rows = tt * nsb
    hist4, hist31 = C4_W - 1, C31_W - 1
    n_lc = D // LANE
    j = pl.program_id(1)
    last = j == pl.num_programs(1) - 1

    @pl.when((pl.program_id(0) == 0) & (j == 0))
    def _():
        for k in range(C4_W):
            w4b[k] = jnp.broadcast_to(c4w_ref[pl.ds(k, 1), :], (SUB, D))
        for k in range(C31_W):
            for lc in range(n_lc):
                w31b[lc, k] = jnp.broadcast_to(c31w_ref[pl.ds(k, 1), pl.ds(lc * LANE, LANE)],
                                               (SUB, LANE))
        for lc in range(n_lc):
            w31b[lc, C31_W] = jnp.broadcast_to(c31b_ref[:, pl.ds(lc * LANE, LANE)], (SUB, LANE))

    @pl.when(j == 0)
    def _():
        zb[pl.ds(0, hist4)] = h4_ref[...].reshape(hist4, g, SUB, D)
        for lc in range(n_lc):
            cb[lc, pl.ds(0, hist31)] = h31_ref[:, :, pl.ds(lc * LANE, LANE)].reshape(
                hist31, g, SUB, LANE)
        h_carry[...] = h0_ref[...].reshape(g, SUB, D)

    x = x_ref[...].reshape(rows, D)
    h = _rms(x, gpre_ref[...]).astype(BF)

    zb[pl.ds(hist4, tt)] = _dot(h, win_ref[:, pl.ds(0, D)]).reshape(tt, g, SUB, D)
    xc = jnp.broadcast_to(c4b_ref[...].reshape(1, 1, 1, D), (tt, g, SUB, D))
    for k in range(C4_W):
        xc = xc + w4b[k][None, None] * zb[pl.ds(k, tt)]

    zb[pl.ds(0, hist4)] = zb[pl.ds(tt, hist4)]

    xcb = xc.reshape(rows, D).astype(BF)
    xc2 = xc.reshape(rows, D)
    for p in range(N_PAIR):
        sl = slice(p * PAIR_W, (p + 1) * PAIR_W)
        z_v = _dot(h, win_ref[:, pl.ds(2 * D + p * PAIR_W, PAIR_W)])
        z_gate = _dot(h, win_ref[:, pl.ds(3 * D + p * PAIR_W, PAIR_W)])
        g2 = _dot(xcb[:, sl], wax_ref[p])
        a, u = _gate_math(g2[:, :PAIR_W] + ba_ref[:, sl], g2[:, PAIR_W:] + bx_ref[:, sl],
                          lam_ref[:, sl], xc2[:, sl])
        a_buf[:, :, :, sl] = a.reshape(tt, g, SUB, PAIR_W)
        u_buf[:, :, :, sl] = u.reshape(tt, g, SUB, PAIR_W)
        gelu_buf[:, sl] = _gelu_tanh(_dot(h, win_ref[:, pl.ds(D + p * PAIR_W, PAIR_W)]))
        glu = (z_v * _sigmoid(z_gate)).reshape(tt, g, SUB, PAIR_W)
        for i_ in range(PAIR_W // LANE):
            cb[p * (PAIR_W // LANE) + i_, pl.ds(hist31, tt)] = glu[..., i_ * LANE:(i_ + 1) * LANE]

    def scan_body(t, hc):
        hc = a_buf[t] * hc + u_buf[t]
        u_buf[t] = hc
        return hc

    h_carry[...] = lax.fori_loop(0, tt, scan_body, h_carry[...])

    rb = min(tt, CONV_CHUNK)

    def conv_body(lc, carry):
        for k0 in range(0, C31_W, CONV_TAPS):
            nk = min(CONV_TAPS, C31_W - k0)
            wk = [w31b[lc, k0 + i] for i in range(nk)]
            for gi in range(g):
                for t0 in range(0, tt, rb):
                    if k0 == 0:
                        acc = [w31b[lc, C31_W] for _ in range(rb)]
                    else:
                        acc = [c_buf[lc, t0 + r, gi] for r in range(rb)]
                    for s in range(rb + nk - 1):
                        xs = cb[lc, t0 + k0 + s, gi]
                        for r in range(rb):
                            if 0 <= s - r < nk:
                                acc[r] = acc[r] + wk[s - r] * xs
                    for r in range(rb):
                        c_buf[lc, t0 + r, gi] = acc[r]
        return carry

    lax.fori_loop(0, n_lc, conv_body, 0)

    for lc in range(n_lc):
        for s in range(hist31):
            cb[lc, s] = cb[lc, tt + s]

    y_lru = (gelu_buf[...] * u_buf[...].reshape(rows, D)).astype(BF)
    out_lru = _dot(y_lru, wout_ref[pl.ds(0, D), :])
    c = jnp.concatenate([c_buf[lc].reshape(rows, LANE) for lc in range(n_lc)], axis=-1)
    yc = _layernorm(c, lng_ref[...], lnb_ref[...])
    y_conf = (yc * _sigmoid(yc)).astype(BF)
    th_ = tt // 2
    for hf in range(2):
        rs = slice(hf * th_ * nsb, (hf + 1) * th_ * nsb)
        out = out_lru[rs] + _dot(y_conf[rs], wout_ref[pl.ds(D, D), :])
        x2 = x[rs] + _rms(out, gpost_ref[...])
        o_ref[:, pl.ds(hf * th_, th_), :] = jnp.swapaxes(x2.reshape(th_, nsb, D), 0, 1)
        if with_q:
            q = _dot(_rms(x2, gq_ref[...]).astype(BF), wq_ref[...])
            q_ref[:, pl.ds(hf * th_, th_), :] = jnp.swapaxes(q.reshape(th_, nsb, D), 0, 1)

    @pl.when(last)
    def _():
        nc4_ref[...] = zb[pl.ds(0, hist4)].reshape(hist4, nsb, D)
        nh_ref[...] = h_carry[...].reshape(nsb, D)
        for lc in range(n_lc):
            nc31_ref[:, :, pl.ds(lc * LANE, LANE)] = cb[lc, pl.ds(0, hist31)].reshape(
                hist31, nsb, LANE)


def _mixer_call(x_tm, hist4, h0, hist31, mix, q_params, nsb, tt):
    t, ns, _ = x_tm.shape
    with_q = q_params is not None
    vec = _const_spec((1, D))
    seq_major = pl.BlockSpec((nsb, tt, D), lambda i, j: (i, j, 0))
    in_specs = [pl.BlockSpec((tt, nsb, D), lambda i, j: (j, i, 0)),
                pl.BlockSpec((C4_W - 1, nsb, D), lambda i, j: (0, i, 0)),
                pl.BlockSpec((nsb, D), lambda i, j: (i, 0)),
                pl.BlockSpec((C31_W - 1, nsb, D), lambda i, j: (0, i, 0)),
                vec, _const_spec((D, 4 * D)), _const_spec((C4_W, D)), vec,
                _const_spec((N_PAIR, PAIR_W, 2 * PAIR_W)), vec, vec, vec,
                _const_spec((C31_W, D)), vec, vec, vec, _const_spec((2 * D, D)), vec]
    out_specs = [seq_major]
    out_shape = [jax.ShapeDtypeStruct((ns, t, D), F32)]
    args = [x_tm, hist4, h0, hist31, *mix]
    if with_q:
        in_specs += [vec, _const_spec((D, D))]
        out_specs.append(seq_major)
        out_shape.append(jax.ShapeDtypeStruct((ns, t, D), F32))
        args += list(q_params)
    out_specs += [pl.BlockSpec((C4_W - 1, nsb, D), lambda i, j: (0, i, 0)),
                  pl.BlockSpec((nsb, D), lambda i, j: (i, 0)),
                  pl.BlockSpec((C31_W - 1, nsb, D), lambda i, j: (0, i, 0))]
    out_shape += [jax.ShapeDtypeStruct((C4_W - 1, ns, D), F32),
                  jax.ShapeDtypeStruct((ns, D), F32),
                  jax.ShapeDtypeStruct((C31_W - 1, ns, D), F32)]
    g = nsb // SUB
    tile = lambda n: pltpu.VMEM((n, g, SUB, D), F32)
    col = lambda n: pltpu.VMEM((D // LANE, n, g, SUB, LANE), F32)
    return pl.pallas_call(
        functools.partial(_mixer_kernel, nsb=nsb, tt=tt, with_q=with_q),
        grid=(ns // nsb, t // tt),
        in_specs=in_specs, out_specs=out_specs, out_shape=out_shape,
        scratch_shapes=[tile(C4_W - 1 + tt), col(C31_W - 1 + tt), tile(tt), tile(tt), col(tt),
                        pltpu.VMEM((tt * nsb, D), F32),
                        pltpu.VMEM((g, SUB, D), F32),
                        pltpu.VMEM((C4_W, SUB, D), F32),
                        pltpu.VMEM((D // LANE, C31_W + 1, SUB, LANE), F32)],
        compiler_params=_params(2),
        name="mixer",
    )(*args)


def _memkv_kernel(m_ref, g_ref, wk_ref, wv_ref, k5_ref, v5_ref, kb_ref, vb_ref, *, nb):
    m = _rms(m_ref[...], g_ref[...]).astype(BF)
    k = _dot(m, wk_ref[...])
    v = _dot(m, wv_ref[...])
    k5_ref[...] = k.reshape(nb, N_MEM, N_HEADS, HEAD_D)
    v5_ref[...] = v.reshape(nb, N_MEM, N_HEADS, HEAD_D)
    kb_ref[...] = k.astype(BF)
    vb_ref[...] = v.astype(BF)


def _memkv_call(mem2d, g, wk, wv, nb):
    rows = mem2d.shape[0]
    b = rows // N_MEM
    tm = nb * N_MEM
    kv5 = pl.BlockSpec((nb, N_MEM, N_HEADS, HEAD_D), lambda i: (i, 0, 0, 0))
    row = pl.BlockSpec((tm, D), lambda i: (i, 0))
    return pl.pallas_call(
        functools.partial(_memkv_kernel, nb=nb),
        grid=(b // nb,),
        in_specs=[row, _const_spec((1, D)), _const_spec((D, D)), _const_spec((D, D))],
        out_specs=[kv5, kv5, row, row],
        out_shape=[jax.ShapeDtypeStruct((b, N_MEM, N_HEADS, HEAD_D), F32),
                   jax.ShapeDtypeStruct((b, N_MEM, N_HEADS, HEAD_D), F32),
                   jax.ShapeDtypeStruct((rows, D), BF), jax.ShapeDtypeStruct((rows, D), BF)],
        compiler_params=_params(1),
        name="memkv",
    )(mem2d, g, wk, wv)


def _attn_ffn_prompt_kernel(x_ref, k_ref, v_ref, gq_ref, wq_ref, wo_ref, gao_ref,
                            gpre_ref, wg_ref, wu_ref, wd_ref, gpost_ref, o_ref):
    x = x_ref[...]
    q = _dot(_rms(x, gq_ref[...]).astype(BF), wq_ref[...])
    k = k_ref[...]
    v = v_ref[...]
    outs = []
    for hh in range(N_HEADS):
        sl = slice(hh * HEAD_D, (hh + 1) * HEAD_D)
        s = lax.dot_general(q[:, sl].astype(BF), k[:, sl], (((1,), (1,)), ((), ())),
                            preferred_element_type=F32) * (HEAD_D ** -0.5)
        e = jnp.exp(s - jnp.max(s, axis=-1, keepdims=True))
        p = e / jnp.sum(e, axis=-1, keepdims=True)
        outs.append(_dot(p.astype(BF), v[:, sl]))
    o = jnp.concatenate(outs, axis=-1)
    x3 = x + _rms(_dot(o.astype(BF), wo_ref[...]), gao_ref[...])
    o_ref[...] = _ffn(x3, gpre_ref[...], wg_ref, wu_ref, wd_ref, gpost_ref[...])


def _attn_ffn_prompt_call(x3d, k3d, v3d, gq, wq, wo, gao, gpre, wg, wu, wd, gpost, tm):
    b, s, _ = x3d.shape
    vec = _const_spec((1, D))
    return pl.pallas_call(
        _attn_ffn_prompt_kernel,
        grid=(b, s // tm),
        in_specs=[pl.BlockSpec((None, tm, D), lambda i, j: (i, j, 0)),
                  pl.BlockSpec((None, N_MEM, D), lambda i, j: (i, 0, 0)),
                  pl.BlockSpec((None, N_MEM, D), lambda i, j: (i, 0, 0)),
                  vec, _const_spec((D, D)), _const_spec((D, D)), vec,
                  vec, _const_spec((D, D_FF)), _const_spec((D, D_FF)), _const_spec((D_FF, D)), vec],
        out_specs=pl.BlockSpec((None, tm, D), lambda i, j: (i, j, 0)),
        out_shape=jax.ShapeDtypeStruct((b, s, D), F32),
        compiler_params=_params(2),
        name="attn_ffn_prompt",
    )(x3d, k3d, v3d, gq, wq, wo, gao, gpre, wg, wu, wd, gpost)


def _attn_sample_kernel(q_ref, k_ref, v_ref, o_ref, *, sq):
    rows = N_HEADS * SUB
    cols = N_MEM * N_HEADS
    own_head = (lax.broadcasted_iota(jnp.int32, (rows, cols), 1) % N_HEADS
                == lax.broadcasted_iota(jnp.int32, (rows, cols), 0) // SUB)
    scores = []
    for s in range(sq):
        kn = k_ref[s].reshape(cols, HEAD_D).astype(BF)
        q = q_ref[pl.ds(s * SUB, SUB), :]
        qh = jnp.concatenate([q[:, h * HEAD_D:(h + 1) * HEAD_D] for h in range(N_HEADS)], axis=0)
        scores.append(lax.dot_general(qh.astype(BF), kn, (((1,), (1,)), ((), ())),
                                      preferred_element_type=F32) * (HEAD_D ** -0.5))
    probs = []
    for s in range(sq):
        sc = jnp.where(own_head, scores[s], -1e30)
        e = jnp.exp(sc - jnp.max(sc, axis=-1, keepdims=True))
        probs.append((e / jnp.sum(e, axis=-1, keepdims=True)).astype(BF))
    for s in range(sq):
        vn = v_ref[s].reshape(cols, HEAD_D).astype(BF)
        oh = _dot(probs[s], vn)
        for h in range(N_HEADS):
            o_ref[pl.ds(s * SUB, SUB), pl.ds(h * HEAD_D, HEAD_D)] = oh[h * SUB:(h + 1) * SUB, :]


def _attn_sample_call(q2d, k5, v5, sq):
    n_seq = k5.shape[1]
    rows = sq * SUB
    kv = pl.BlockSpec((None, sq, N_MEM, N_HEADS, HEAD_D), lambda i: (0, i, 0, 0, 0))
    return pl.pallas_call(
        functools.partial(_attn_sample_kernel, sq=sq),
        grid=(n_seq // sq,),
        in_specs=[pl.BlockSpec((rows, D), lambda i: (i, 0)), kv, kv],
        out_specs=pl.BlockSpec((rows, D), lambda i: (i, 0)),
        out_shape=jax.ShapeDtypeStruct((n_seq * SUB, D), F32),
        compiler_params=_params(1),
        name="attn_sample",
    )(q2d, k5, v5)


def _oproj_ffn_kernel(x_ref, a_ref, wo_ref, gao_ref, gpre_ref, wg_ref, wu_ref, wd_ref, gpost_ref,
                      o_ref):
    x3 = x_ref[...] + _rms(_dot(a_ref[...].astype(BF), wo_ref[...]), gao_ref[...])
    o_ref[...] = _ffn(x3, gpre_ref[...], wg_ref, wu_ref, wd_ref, gpost_ref[...])


def _oproj_ffn_call(x2d, a2d, wo, gao, gpre, wg, wu, wd, gpost, tm):
    rows = x2d.shape[0]
    vec = _const_spec((1, D))
    row = pl.BlockSpec((tm, D), lambda i: (i, 0))
    return pl.pallas_call(
        _oproj_ffn_kernel,
        grid=(rows // tm,),
        in_specs=[row, row, _const_spec((D, D)), vec,
                  vec, _const_spec((D, D_FF)), _const_spec((D, D_FF)), _const_spec((D_FF, D)), vec],
        out_specs=row,
        out_shape=jax.ShapeDtypeStruct((rows, D), F32),
        compiler_params=_params(1),
        name="oproj_ffn",
    )(x2d, a2d, wo, gao, gpre, wg, wu, wd, gpost)


def _pair_gate_weights(w_a, w_x):
    hd = w_a.shape[-1]

    def blockdiag(w):
        w = w.reshape(N_PAIR, 2, hd, hd)
        z = jnp.zeros((N_PAIR, hd, hd), w.dtype)
        top = jnp.concatenate([w[:, 0], z], axis=-1)
        bot = jnp.concatenate([z, w[:, 1]], axis=-1)
        return jnp.concatenate([top, bot], axis=-2)

    return jnp.concatenate([blockdiag(w_a), blockdiag(w_x)], axis=-1).astype(BF)


def _seq_first(a):
    return jnp.swapaxes(a, 0, 1)


def _layer(l, xp, xs, mem_prompt, state_lru_conv, state_lru_h, state_conf_conv, cache_mem_k,
           cache_mem_v, w):
    bp, sp, _ = xp.shape
    bs, ss, _ = xs.shape
    vec = lambda name: w[name][l].reshape(1, D)
    bf = lambda name: w[name][l].astype(BF)

    ff1 = (vec('g_ff1_pre'), bf('ff1_w_gate'), bf('ff1_w_up'), bf('ff1_w_down'), vec('g_ff1_post'))
    ff2 = (vec('g_ff2_pre'), bf('ff2_w_gate'), bf('ff2_w_up'), bf('ff2_w_down'), vec('g_ff2_post'))
    mix = (vec('g_mix_pre'), bf('w_in'), w['lru_conv_w'][l], vec('lru_conv_b'),
           _pair_gate_weights(w['lru_w_a'][l], w['lru_w_x'][l]), vec('lru_b_a'), vec('lru_b_x'),
           vec('lru_lambda'), w['conf_conv_w'][l], vec('conf_conv_b'), vec('conf_ln_g'),
           vec('conf_ln_b'), bf('w_out'), vec('g_mix_post'))
    wq, wo = bf('w_q'), bf('w_o')

    mk5, mv5, mkb, mvb = _memkv_call(mem_prompt.reshape(bp * N_MEM, D), vec('g_mem_kv'),
                                     bf('w_mem_k'), bf('w_mem_v'), nb=MEMKV_BATCHES)
    x1 = _ffn_tmajor_call(xp, *ff1, nsb=bp, tt=ROW_TILE // bp)
    x2, c4p, hp, c31p = _mixer_call(
        x1, jnp.zeros((C4_W - 1, bp, D), F32), jnp.zeros((bp, D), F32),
        jnp.zeros((C31_W - 1, bp, D), F32), mix, None, nsb=bp, tt=PROMPT_TIME_TILE)
    xp_out = _attn_ffn_prompt_call(x2, mkb.reshape(bp, N_MEM, D), mvb.reshape(bp, N_MEM, D),
                                   vec('g_mem_pre'), wq, wo, vec('g_mem_post'), *ff2, tm=ROW_TILE)

    s1 = _ffn_tmajor_call(xs, *ff1, nsb=ROW_TILE // ss, tt=ss)
    s2, q, c4s, hs, c31s = _mixer_call(
        s1, _seq_first(state_lru_conv[l]), state_lru_h[l], _seq_first(state_conf_conv[l]), mix,
        (vec('g_mem_pre'), wq), nsb=SAMPLE_MIXER_SEQS, tt=ss)
    att = _attn_sample_call(q.reshape(bs * ss, D), cache_mem_k[l:l + 1], cache_mem_v[l:l + 1],
                            sq=SAMPLE_ATTN_SEQS)
    xs_out = _oproj_ffn_call(s2.reshape(bs * ss, D), att, wo, vec('g_mem_post'), *ff2,
                             tm=ROW_TILE).reshape(bs, ss, D)

    return (xp_out, xs_out, _seq_first(c4p), hp, _seq_first(c31p), mk5, mv5,
            _seq_first(c4s), hs, _seq_first(c31s))


def kernel(x_prompt, x_sample, mem_prompt, state_lru_conv, state_lru_h, state_conf_conv, cache_mem_k, cache_mem_v, g_ff1_pre, ff1_w_gate, ff1_w_up, ff1_w_down, g_ff1_post, g_mix_pre, w_in, lru_conv_w, lru_conv_b, lru_w_a, lru_b_a, lru_w_x, lru_b_x, lru_lambda, conf_conv_w, conf_conv_b, conf_ln_g, conf_ln_b, w_out, g_mix_post, g_mem_pre, g_mem_kv, w_mem_k, w_mem_v, w_q, w_o, g_mem_post, g_ff2_pre, ff2_w_gate, ff2_w_up, ff2_w_down, g_ff2_post):
    w = dict(g_ff1_pre=g_ff1_pre, ff1_w_gate=ff1_w_gate, ff1_w_up=ff1_w_up, ff1_w_down=ff1_w_down,
             g_ff1_post=g_ff1_post, g_mix_pre=g_mix_pre, w_in=w_in, lru_conv_w=lru_conv_w,
             lru_conv_b=lru_conv_b, lru_w_a=lru_w_a, lru_b_a=lru_b_a, lru_w_x=lru_w_x,
             lru_b_x=lru_b_x, lru_lambda=lru_lambda, conf_conv_w=conf_conv_w,
             conf_conv_b=conf_conv_b, conf_ln_g=conf_ln_g, conf_ln_b=conf_ln_b, w_out=w_out,
             g_mix_post=g_mix_post, g_mem_pre=g_mem_pre, g_mem_kv=g_mem_kv, w_mem_k=w_mem_k,
             w_mem_v=w_mem_v, w_q=w_q, w_o=w_o, g_mem_post=g_mem_post, g_ff2_pre=g_ff2_pre,
             ff2_w_gate=ff2_w_gate, ff2_w_up=ff2_w_up, ff2_w_down=ff2_w_down, g_ff2_post=g_ff2_post)
    depth = w_in.shape[0]
    xp, xs = x_prompt, x_sample
    per_layer = []
    for l in range(depth):
        outs = _layer(l, xp, xs, mem_prompt, state_lru_conv, state_lru_h, state_conf_conv,
                      cache_mem_k, cache_mem_v, w)
        xp, xs = outs[0], outs[1]
        per_layer.append(outs[2:])
    stacked = [jnp.stack([pl_[i] for pl_ in per_layer]) for i in range(8)]
    return (xp, xs, *stacked)
```

```python
import functools

import jax
import jax.numpy as jnp
from jax import lax
from jax.experimental import pallas as pl
from jax.experimental.pallas import tpu as pltpu

D = 1024
D_FF = 2816
N_PAIR = 4
PAIR_W = 256
LRU_C = 8.0
C4_W = 4
C31_W = 31
N_MEM = 256
N_HEADS = 4
HEAD_D = 256
FFN_RES = 0.5
EPS = 1e-6
SUB = 8
LANE = 128
CONV_CHUNK = 8
CONV_TAPS = 16
ROW_TILE = 512
PROMPT_TIME_TILE = 64
SAMPLE_MIXER_SEQS = 32
SAMPLE_ATTN_SEQS = 8
MEMKV_BATCHES = 2
VMEM_LIMIT = 56 * 1024 * 1024

BF = jnp.bfloat16
F32 = jnp.float32


def _dot(a, b):
    return jnp.dot(a, b, preferred_element_type=F32)


def _rms(x, g):
    ms = jnp.mean(x * x, axis=-1, keepdims=True)
    return x * lax.rsqrt(ms + EPS) * g


def _sigmoid(x):
    return 1.0 / (1.0 + jnp.exp(-x))


def _gelu_tanh(x):
    return 0.5 * x * (1.0 + jnp.tanh(0.7978845608028654 * (x + 0.044715 * (x * x * x))))


def _ffn(x, gpre, wg_ref, wu_ref, wd_ref, gpost):
    h = _rms(x, gpre).astype(BF)
    gate = _dot(h, wg_ref[...])
    up = _dot(h, wu_ref[...])
    act = (gate * _sigmoid(gate) * up).astype(BF)
    y = _dot(act, wd_ref[...])
    return x + FFN_RES * _rms(y, gpost)


def _gate_math(r_pre, i_pre, lam, xc):
    r = _sigmoid(r_pre)
    i = _sigmoid(i_pre)
    softplus_neg_lam = jnp.maximum(-lam, 0.0) + jnp.log1p(jnp.exp(-jnp.abs(lam)))
    log_a = (-LRU_C * r) * softplus_neg_lam
    a = jnp.exp(log_a)
    th = jnp.tanh(log_a)
    y = (-2.0 * th) / (1.0 - th)
    root = jnp.where(y == 0.0, 0.0, y * lax.rsqrt(y))
    return a, root * (i * xc)


def _layernorm(x, g, b):
    xc = x - jnp.mean(x, axis=-1, keepdims=True)
    return xc * lax.rsqrt(jnp.mean(xc * xc, axis=-1, keepdims=True) + EPS) * g + b


def _const_spec(shape):
    n = len(shape)
    return pl.BlockSpec(shape, lambda *_: (0,) * n, pipeline_mode=pl.Buffered(1))


def _params(n_grid):
    return pltpu.CompilerParams(dimension_semantics=("arbitrary",) * n_grid,
                                vmem_limit_bytes=VMEM_LIMIT)


def _ffn_tmajor_kernel(x_ref, gpre_ref, wg_ref, wu_ref, wd_ref, gpost_ref, o_ref, *, nsb, tt):
    hs = nsb // 2
    for i in range(2):
        x = x_ref[pl.ds(i * hs, hs)].reshape(hs * tt, D)
        y = _ffn(x, gpre_ref[...], wg_ref, wu_ref, wd_ref, gpost_ref[...])
        o_ref[:, pl.ds(i * hs, hs), :] = jnp.swapaxes(y.reshape(hs, tt, D), 0, 1)


def _ffn_tmajor_call(x3d, gpre, wg, wu, wd, gpost, nsb, tt):
    ns, t, _ = x3d.shape
    return pl.pallas_call(
        functools.partial(_ffn_tmajor_kernel, nsb=nsb, tt=tt),
        grid=(ns // nsb, t // tt),
        in_specs=[pl.BlockSpec((nsb, tt, D), lambda i, j: (i, j, 0)),
                  _const_spec((1, D)), _const_spec((D, D_FF)), _const_spec((D, D_FF)),
                  _const_spec((D_FF, D)), _const_spec((1, D))],
        out_specs=pl.BlockSpec((tt, nsb, D), lambda i, j: (j, i, 0)),
        out_shape=jax.ShapeDtypeStruct((t, ns, D), F32),
        compiler_params=_params(2),
        name="ffn1",
    )(x3d, gpre, wg, wu, wd, gpost)


def _mixer_kernel(*refs, nsb, tt, with_q):
    (x_ref, h4_ref, h0_ref, h31_ref, gpre_ref, win_ref, c4w_ref, c4b_ref, wax_ref, ba_ref, bx_ref,
     lam_ref, c31w_ref, c31b_ref, lng_ref, lnb_ref, wout_ref, gpost_ref) = refs[:18]
    refs = refs[18:]
    if with_q:
        gq_ref, wq_ref, o_ref, q_ref = refs[:4]
        refs = refs[4:]
    else:
        o_ref = refs[0]
        refs = refs[1:]
    nc4_ref, nh_ref, nc31_ref, zb, cb, a_buf, u_buf, c_buf, gelu_buf, h_carry, w4b, w31b = refs

    g = nsb // SUB
    rows = tt * nsb
    hist4, hist31 = C4_W - 1, C31_W - 1
    n_lc = D // LANE
    j = pl.program_id(1)
    last = j == pl.num_programs(1) - 1

    @pl.when((pl.program_id(0) == 0) & (j == 0))
    def _():
        for k in range(C4_W):
            w4b[k] = jnp.broadcast_to(c4w_ref[pl.ds(k, 1), :], (SUB, D))
        for k in range(C31_W):
            for lc in range(n_lc):
                w31b[lc, k] = jnp.broadcast_to(c31w_ref[pl.ds(k, 1), pl.ds(lc * LANE, LANE)],
                                               (SUB, LANE))
        for lc in range(n_lc):
            w31b[lc, C31_W] = jnp.broadcast_to(c31b_ref[:, pl.ds(lc * LANE, LANE)], (SUB, LANE))

    @pl.when(j == 0)
    def _():
        zb[pl.ds(0, hist4)] = h4_ref[...].reshape(hist4, g, SUB, D)
        for lc in range(n_lc):
            cb[lc, pl.ds(0, hist31)] = h31_ref[:, :, pl.ds(lc * LANE, LANE)].reshape(
                hist31, g, SUB, LANE)
        h_carry[...] = h0_ref[...].reshape(g, SUB, D)

    x = x_ref[...].reshape(rows, D)
    h = _rms(x, gpre_ref[...]).astype(BF)

    zb[pl.ds(hist4, tt)] = _dot(h, win_ref[:, pl.ds(0, D)]).reshape(tt, g, SUB, D)
    xc = jnp.broadcast_to(c4b_ref[...].reshape(1, 1, 1, D), (tt, g, SUB, D))
    for k in range(C4_W):
        xc = xc + w4b[k][None, None] * zb[pl.ds(k, tt)]

    zb[pl.ds(0, hist4)] = zb[pl.ds(tt, hist4)]

    xcb = xc.reshape(rows, D).astype(BF)
    xc2 = xc.reshape(rows, D)
    for p in range(N_PAIR):
        sl = slice(p * PAIR_W, (p + 1) * PAIR_W)
        z_v = _dot(h, win_ref[:, pl.ds(2 * D + p * PAIR_W, PAIR_W)])
        z_gate = _dot(h, win_ref[:, pl.ds(3 * D + p * PAIR_W, PAIR_W)])
        g2 = _dot(xcb[:, sl], wax_ref[p])
        a, u = _gate_math(g2[:, :PAIR_W] + ba_ref[:, sl], g2[:, PAIR_W:] + bx_ref[:, sl],
                          lam_ref[:, sl], xc2[:, sl])
        a_buf[:, :, :, sl] = a.reshape(tt, g, SUB, PAIR_W)
        u_buf[:, :, :, sl] = u.reshape(tt, g, SUB, PAIR_W)
        gelu_buf[:, sl] = _gelu_tanh(_dot(h, win_ref[:, pl.ds(D + p * PAIR_W, PAIR_W)]))
        glu = (z_v * _sigmoid(z_gate)).reshape(tt, g, SUB, PAIR_W)
        for i_ in range(PAIR_W // LANE):
            cb[p * (PAIR_W // LANE) + i_, pl.ds(hist31, tt)] = glu[..., i_ * LANE:(i_ + 1) * LANE]

    def scan_body(t, hc):
        hc = a_buf[t] * hc + u_buf[t]
        u_buf[t] = hc
        return hc

    h_carry[...] = lax.fori_loop(0, tt, scan_body, h_carry[...])

    rb = min(tt, CONV_CHUNK)

    def conv_body(lc, carry):
        for k0 in range(0, C31_W, CONV_TAPS):
            nk = min(CONV_TAPS, C31_W - k0)
            wk = [w31b[lc, k0 + i] for i in range(nk)]
            for gi in range(g):
                for t0 in range(0, tt, rb):
                    if k0 == 0:
                        acc = [w31b[lc, C31_W] for _ in range(rb)]
                    else:
                        acc = [c_buf[lc, t0 + r, gi] for r in range(rb)]
                    for s in range(rb + nk - 1):
                        xs = cb[lc, t0 + k0 + s, gi]
                        for r in range(rb):
                            if 0 <= s - r < nk:
                                acc[r] = acc[r] + wk[s - r] * xs
                    for r in range(rb):
                        c_buf[lc, t0 + r, gi] = acc[r]
        return carry

    lax.fori_loop(0, n_lc, conv_body, 0)

    for lc in range(n_lc):
        for s in range(hist31):
            cb[lc, s] = cb[lc, tt + s]

    y_lru = (gelu_buf[...] * u_buf[...].reshape(rows, D)).astype(BF)
    out_lru = _dot(y_lru, wout_ref[pl.ds(0, D), :])
    c = jnp.concatenate([c_buf[lc].reshape(rows, LANE) for lc in range(n_lc)], axis=-1)
    yc = _layernorm(c, lng_ref[...], lnb_ref[...])
    y_conf = (yc * _sigmoid(yc)).astype(BF)
    th_ = tt // 2
    for hf in range(2):
        rs = slice(hf * th_ * nsb, (hf + 1) * th_ * nsb)
        out = out_lru[rs] + _dot(y_conf[rs], wout_ref[pl.ds(D, D), :])
        x2 = x[rs] + _rms(out, gpost_ref[...])
        o_ref[:, pl.ds(hf * th_, th_), :] = jnp.swapaxes(x2.reshape(th_, nsb, D), 0, 1)
        if with_q:
            q = _dot(_rms(x2, gq_ref[...]).astype(BF), wq_ref[...])
            q_ref[:, pl.ds(hf * th_, th_), :] = jnp.swapaxes(q.reshape(th_, nsb, D), 0, 1)

    @pl.when(last)
    def _():
        nc4_ref[...] = zb[pl.ds(0, hist4)].reshape(hist4, nsb, D)
        nh_ref[...] = h_carry[...].reshape(nsb, D)
        for lc in range(n_lc):
            nc31_ref[:, :, pl.ds(lc * LANE, LANE)] = cb[lc, pl.ds(0, hist31)].reshape(
                hist31, nsb, LANE)


def _mixer_call(x_tm, hist4, h0, hist31, mix, q_params, nsb, tt):
    t, ns, _ = x_tm.shape
    with_q = q_params is not None
    vec = _const_spec((1, D))
    seq_major = pl.BlockSpec((nsb, tt, D), lambda i, j: (i, j, 0))
    in_specs = [pl.BlockSpec((tt, nsb, D), lambda i, j: (j, i, 0)),
                pl.BlockSpec((C4_W - 1, nsb, D), lambda i, j: (0, i, 0)),
                pl.BlockSpec((nsb, D), lambda i, j: (i, 0)),
                pl.BlockSpec((C31_W - 1, nsb, D), lambda i, j: (0, i, 0)),
                vec, _const_spec((D, 4 * D)), _const_spec((C4_W, D)), vec,
                _const_spec((N_PAIR, PAIR_W, 2 * PAIR_W)), vec, vec, vec,
                _const_spec((C31_W, D)), vec, vec, vec, _const_spec((2 * D, D)), vec]
    out_specs = [seq_major]
    out_shape = [jax.ShapeDtypeStruct((ns, t, D), F32)]
    args = [x_tm, hist4, h0, hist31, *mix]
    if with_q:
        in_specs += [vec, _const_spec((D, D))]
        out_specs.append(seq_major)
        out_shape.append(jax.ShapeDtypeStruct((ns, t, D), F32))
        args += list(q_params)
    out_specs += [pl.BlockSpec((C4_W - 1, nsb, D), lambda i, j: (0, i, 0)),
                  pl.BlockSpec((nsb, D), lambda i, j: (i, 0)),
                  pl.BlockSpec((C31_W - 1, nsb, D), lambda i, j: (0, i, 0))]
    out_shape += [jax.ShapeDtypeStruct((C4_W - 1, ns, D), F32),
                  jax.ShapeDtypeStruct((ns, D), F32),
                  jax.ShapeDtypeStruct((C31_W - 1, ns, D), F32)]
    g = nsb // SUB
    tile = lambda n: pltpu.VMEM((n, g, SUB, D), F32)
    col = lambda n: pltpu.VMEM((D // LANE, n, g, SUB, LANE), F32)
    return pl.pallas_call(
        functools.partial(_mixer_kernel, nsb=nsb, tt=tt, with_q=with_q),
        grid=(ns // nsb, t // tt),
        in_specs=in_specs, out_specs=out_specs, out_shape=out_shape,
        scratch_shapes=[tile(C4_W - 1 + tt), col(C31_W - 1 + tt), tile(tt), tile(tt), col(tt),
                        pltpu.VMEM((tt * nsb, D), F32),
                        pltpu.VMEM((g, SUB, D), F32),
                        pltpu.VMEM((C4_W, SUB, D), F32),
                        pltpu.VMEM((D // LANE, C31_W + 1, SUB, LANE), F32)],
        compiler_params=_params(2),
        name="mixer",
    )(*args)


def _memkv_kernel(m_ref, g_ref, wk_ref, wv_ref, k5_ref, v5_ref, kb_ref, vb_ref, *, nb):
    m = _rms(m_ref[...], g_ref[...]).astype(BF)
    k = _dot(m, wk_ref[...])
    v = _dot(m, wv_ref[...])
    k5_ref[...] = k.reshape(nb, N_MEM, N_HEADS, HEAD_D)
    v5_ref[...] = v.reshape(nb, N_MEM, N_HEADS, HEAD_D)
    kb_ref[...] = k.astype(BF)
    vb_ref[...] = v.astype(BF)


def _memkv_call(mem2d, g, wk, wv, nb):
    rows = mem2d.shape[0]
    b = rows // N_MEM
    tm = nb * N_MEM
    kv5 = pl.BlockSpec((nb, N_MEM, N_HEADS, HEAD_D), lambda i: (i, 0, 0, 0))
    row = pl.BlockSpec((tm, D), lambda i: (i, 0))
    return pl.pallas_call(
        functools.partial(_memkv_kernel, nb=nb),
        grid=(b // nb,),
        in_specs=[row, _const_spec((1, D)), _const_spec((D, D)), _const_spec((D, D))],
        out_specs=[kv5, kv5, row, row],
        out_shape=[jax.ShapeDtypeStruct((b, N_MEM, N_HEADS, HEAD_D), F32),
                   jax.ShapeDtypeStruct((b, N_MEM, N_HEADS, HEAD_D), F32),
                   jax.ShapeDtypeStruct((rows, D), BF), jax.ShapeDtypeStruct((rows, D), BF)],
        compiler_params=_params(1),
        name="memkv",
    )(mem2d, g, wk, wv)


def _attn_ffn_prompt_kernel(x_ref, k_ref, v_ref, gq_ref, wq_ref, wo_ref, gao_ref,
                            gpre_ref, wg_ref, wu_ref, wd_ref, gpost_ref, o_ref):
    x = x_ref[...]
    q = _dot(_rms(x, gq_ref[...]).astype(BF), wq_ref[...])
    k = k_ref[...]
    v = v_ref[...]
    heads = [slice(hh * HEAD_D, (hh + 1) * HEAD_D) for hh in range(N_HEADS)]
    scores = [lax.dot_general(q[:, sl].astype(BF), k[:, sl], (((1,), (1,)), ((), ())),
                              preferred_element_type=F32) * (HEAD_D ** -0.5) for sl in heads]
    probs = []
    for s in scores:
        e = jnp.exp(s - jnp.max(s, axis=-1, keepdims=True))
        probs.append((e / jnp.sum(e, axis=-1, keepdims=True)).astype(BF))
    o = jnp.concatenate([_dot(p, v[:, sl]) for p, sl in zip(probs, heads)], axis=-1)
    x3 = x + _rms(_dot(o.astype(BF), wo_ref[...]), gao_ref[...])
    o_ref[...] = _ffn(x3, gpre_ref[...], wg_ref, wu_ref, wd_ref, gpost_ref[...])


def _attn_ffn_prompt_call(x3d, k3d, v3d, gq, wq, wo, gao, gpre, wg, wu, wd, gpost, tm):
    b, s, _ = x3d.shape
    vec = _const_spec((1, D))
    return pl.pallas_call(
        _attn_ffn_prompt_kernel,
        grid=(b, s // tm),
        in_specs=[pl.BlockSpec((None, tm, D), lambda i, j: (i, j, 0)),
                  pl.BlockSpec((None, N_MEM, D), lambda i, j: (i, 0, 0)),
                  pl.BlockSpec((None, N_MEM, D), lambda i, j: (i, 0, 0)),
                  vec, _const_spec((D, D)), _const_spec((D, D)), vec,
                  vec, _const_spec((D, D_FF)), _const_spec((D, D_FF)), _const_spec((D_FF, D)), vec],
        out_specs=pl.BlockSpec((None, tm, D), lambda i, j: (i, j, 0)),
        out_shape=jax.ShapeDtypeStruct((b, s, D), F32),
        compiler_params=_params(2),
        name="attn_ffn_prompt",
    )(x3d, k3d, v3d, gq, wq, wo, gao, gpre, wg, wu, wd, gpost)


def _attn_sample_kernel(q_ref, k_ref, v_ref, o_ref, *, sq):
    rows = N_HEADS * SUB
    cols = N_MEM * N_HEADS
    own_head = (lax.broadcasted_iota(jnp.int32, (rows, cols), 1) % N_HEADS
                == lax.broadcasted_iota(jnp.int32, (rows, cols), 0) // SUB)
    scores = []
    for s in range(sq):
        kn = k_ref[s].reshape(cols, HEAD_D).astype(BF)
        q = q_ref[pl.ds(s * SUB, SUB), :]
        qh = jnp.concatenate([q[:, h * HEAD_D:(h + 1) * HEAD_D] for h in range(N_HEADS)], axis=0)
        scores.append(lax.dot_general(qh.astype(BF), kn, (((1,), (1,)), ((), ())),
                                      preferred_element_type=F32) * (HEAD_D ** -0.5))
    probs = []
    for s in range(sq):
        sc = jnp.where(own_head, scores[s], -1e30)
        e = jnp.exp(sc - jnp.max(sc, axis=-1, keepdims=True))
        probs.append((e / jnp.sum(e, axis=-1, keepdims=True)).astype(BF))
    for s in range(sq):
        vn = v_ref[s].reshape(cols, HEAD_D).astype(BF)
        oh = _dot(probs[s], vn)
        for h in range(N_HEADS):
            o_ref[pl.ds(s * SUB, SUB), pl.ds(h * HEAD_D, HEAD_D)] = oh[h * SUB:(h + 1) * SUB, :]


def _attn_sample_call(q2d, k5, v5, sq):
    n_seq = k5.shape[1]
    rows = sq * SUB
    kv = pl.BlockSpec((None, sq, N_MEM, N_HEADS, HEAD_D), lambda i: (0, i, 0, 0, 0))
    return pl.pallas_call(
        functools.partial(_attn_sample_kernel, sq=sq),
        grid=(n_seq // sq,),
        in_specs=[pl.BlockSpec((rows, D), lambda i: (i, 0)), kv, kv],
        out_specs=pl.BlockSpec((rows, D), lambda i: (i, 0)),
        out_shape=jax.ShapeDtypeStruct((n_seq * SUB, D), F32),
        compiler_params=_params(1),
        name="attn_sample",
    )(q2d, k5, v5)


def _oproj_ffn_kernel(x_ref, a_ref, wo_ref, gao_ref, gpre_ref, wg_ref, wu_ref, wd_ref, gpost_ref,
                      o_ref):
    x3 = x_ref[...] + _rms(_dot(a_ref[...].astype(BF), wo_ref[...]), gao_ref[...])
    o_ref[...] = _ffn(x3, gpre_ref[...], wg_ref, wu_ref, wd_ref, gpost_ref[...])


def _oproj_ffn_call(x2d, a2d, wo, gao, gpre, wg, wu, wd, gpost, tm):
    rows = x2d.shape[0]
    vec = _const_spec((1, D))
    row = pl.BlockSpec((tm, D), lambda i: (i, 0))
    return pl.pallas_call(
        _oproj_ffn_kernel,
        grid=(rows // tm,),
        in_specs=[row, row, _const_spec((D, D)), vec,
                  vec, _const_spec((D, D_FF)), _const_spec((D, D_FF)), _const_spec((D_FF, D)), vec],
        out_specs=row,
        out_shape=jax.ShapeDtypeStruct((rows, D), F32),
        compiler_params=_params(1),
        name="oproj_ffn",
    )(x2d, a2d, wo, gao, gpre, wg, wu, wd, gpost)


def _pair_gate_weights(w_a, w_x):
    hd = w_a.shape[-1]

    def blockdiag(w):
        w = w.reshape(N_PAIR, 2, hd, hd)
        z = jnp.zeros((N_PAIR, hd, hd), w.dtype)
        top = jnp.concatenate([w[:, 0], z], axis=-1)
        bot = jnp.concatenate([z, w[:, 1]], axis=-1)
        return jnp.concatenate([top, bot], axis=-2)

    return jnp.concatenate([blockdiag(w_a), blockdiag(w_x)], axis=-1).astype(BF)


def _seq_first(a):
    return jnp.swapaxes(a, 0, 1)


def _layer(l, xp, xs, mem_prompt, state_lru_conv, state_lru_h, state_conf_conv, cache_mem_k,
           cache_mem_v, w):
    bp, sp, _ = xp.shape
    bs, ss, _ = xs.shape
    vec = lambda name: w[name][l].reshape(1, D)
    bf = lambda name: w[name][l].astype(BF)

    ff1 = (vec('g_ff1_pre'), bf('ff1_w_gate'), bf('ff1_w_up'), bf('ff1_w_down'), vec('g_ff1_post'))
    ff2 = (vec('g_ff2_pre'), bf('ff2_w_gate'), bf('ff2_w_up'), bf('ff2_w_down'), vec('g_ff2_post'))
    mix = (vec('g_mix_pre'), bf('w_in'), w['lru_conv_w'][l], vec('lru_conv_b'),
           _pair_gate_weights(w['lru_w_a'][l], w['lru_w_x'][l]), vec('lru_b_a'), vec('lru_b_x'),
           vec('lru_lambda'), w['conf_conv_w'][l], vec('conf_conv_b'), vec('conf_ln_g'),
           vec('conf_ln_b'), bf('w_out'), vec('g_mix_post'))
    wq, wo = bf('w_q'), bf('w_o')

    mk5, mv5, mkb, mvb = _memkv_call(mem_prompt.reshape(bp * N_MEM, D), vec('g_mem_kv'),
                                     bf('w_mem_k'), bf('w_mem_v'), nb=MEMKV_BATCHES)
    x1 = _ffn_tmajor_call(xp, *ff1, nsb=bp, tt=ROW_TILE // bp)
    x2, c4p, hp, c31p = _mixer_call(
        x1, jnp.zeros((C4_W - 1, bp, D), F32), jnp.zeros((bp, D), F32),
        jnp.zeros((C31_W - 1, bp, D), F32), mix, None, nsb=bp, tt=PROMPT_TIME_TILE)
    xp_out = _attn_ffn_prompt_call(x2, mkb.reshape(bp, N_MEM, D), mvb.reshape(bp, N_MEM, D),
                                   vec('g_mem_pre'), wq, wo, vec('g_mem_post'), *ff2, tm=ROW_TILE)

    s1 = _ffn_tmajor_call(xs, *ff1, nsb=ROW_TILE // ss, tt=ss)
    s2, q, c4s, hs, c31s = _mixer_call(
        s1, _seq_first(state_lru_conv[l]), state_lru_h[l], _seq_first(state_conf_conv[l]), mix,
        (vec('g_mem_pre'), wq), nsb=SAMPLE_MIXER_SEQS, tt=ss)
    att = _attn_sample_call(q.reshape(bs * ss, D), cache_mem_k[l:l + 1], cache_mem_v[l:l + 1],
                            sq=SAMPLE_ATTN_SEQS)
    xs_out = _oproj_ffn_call(s2.reshape(bs * ss, D), att, wo, vec('g_mem_post'), *ff2,
                             tm=ROW_TILE).reshape(bs, ss, D)

    return (xp_out, xs_out, _seq_first(c4p), hp, _seq_first(c31p), mk5, mv5,
            _seq_first(c4s), hs, _seq_first(c31s))


def kernel(x_prompt, x_sample, mem_prompt, state_lru_conv, state_lru_h, state_conf_conv, cache_mem_k, cache_mem_v, g_ff1_pre, ff1_w_gate, ff1_w_up, ff1_w_down, g_ff1_post, g_mix_pre, w_in, lru_conv_w, lru_conv_b, lru_w_a, lru_b_a, lru_w_x, lru_b_x, lru_lambda, conf_conv_w, conf_conv_b, conf_ln_g, conf_ln_b, w_out, g_mix_post, g_mem_pre, g_mem_kv, w_mem_k, w_mem_v, w_q, w_o, g_mem_post, g_ff2_pre, ff2_w_gate, ff2_w_up, ff2_w_down, g_ff2_post):
    w = dict(g_ff1_pre=g_ff1_pre, ff1_w_gate=ff1_w_gate, ff1_w_up=ff1_w_up, ff1_w_down=ff1_w_down,
             g_ff1_post=g_ff1_post, g_mix_pre=g_mix_pre, w_in=w_in, lru_conv_w=lru_conv_w,
             lru_conv_b=lru_conv_b, lru_w_a=lru_w_a, lru_b_a=lru_b_a, lru_w_x=lru_w_x,
             lru_b_x=lru_b_x, lru_lambda=lru_lambda, conf_conv_w=conf_conv_w,
             conf_conv_b=conf_conv_b, conf_ln_g=conf_ln_g, conf_ln_b=conf_ln_b, w_out=w_out,
             g_mix_post=g_mix_post, g_mem_pre=g_mem_pre, g_mem_kv=g_mem_kv, w_mem_k=w_mem_k,
             w_mem_v=w_mem_v, w_q=w_q, w_o=w_o, g_mem_post=g_mem_post, g_ff2_pre=g_ff2_pre,
             ff2_w_gate=ff2_w_gate, ff2_w_up=ff2_w_up, ff2_w_down=ff2_w_down, g_ff2_post=g_ff2_post)
    depth = w_in.shape[0]
    xp, xs = x_prompt, x_sample
    per_layer = []
    for l in range(depth):
        outs = _layer(l, xp, xs, mem_prompt, state_lru_conv, state_lru_h, state_conf_conv,
                      cache_mem_k, cache_mem_v, w)
        xp, xs = outs[0], outs[1]
        per_layer.append(outs[2:])
    stacked = [jnp.stack([pl_[i] for pl_ in per_layer]) for i in range(8)]
    return (xp, xs, *stacked)
```

```python
import functools

import jax
import jax.numpy as jnp
from jax import lax
from jax.experimental import pallas as pl
from jax.experimental.pallas import tpu as pltpu

D = 1024
D_FF = 2816
N_PAIR = 4
PAIR_W = 256
LRU_C = 8.0
C4_W = 4
C31_W = 31
N_MEM = 256
N_HEADS = 4
HEAD_D = 256
FFN_RES = 0.5
EPS = 1e-6
SUB = 8
LANE = 128
CONV_CHUNK = 8
CONV_TAPS = 16
ROW_TILE = 512
PROMPT_TIME_TILE = 64
SAMPLE_MIXER_SEQS = 32
SAMPLE_ATTN_SEQS = 8
MEMKV_BATCHES = 2
VMEM_LIMIT = 56 * 1024 * 1024

BF = jnp.bfloat16
F32 = jnp.float32


def _dot(a, b):
    return jnp.dot(a, b, preferred_element_type=F32)


def _rms(x, g):
    ms = jnp.mean(x * x, axis=-1, keepdims=True)
    return x * lax.rsqrt(ms + EPS) * g


def _sigmoid(x):
    return 1.0 / (1.0 + jnp.exp(-x))


def _gelu_tanh(x):
    return 0.5 * x * (1.0 + jnp.tanh(0.7978845608028654 * (x + 0.044715 * (x * x * x))))


def _ffn(x, gpre, wg_ref, wu_ref, wd_ref, gpost):
    h = _rms(x, gpre).astype(BF)
    gate = _dot(h, wg_ref[...])
    up = _dot(h, wu_ref[...])
    act = (gate * _sigmoid(gate) * up).astype(BF)
    y = _dot(act, wd_ref[...])
    return x + FFN_RES * _rms(y, gpost)


def _gate_math(r_pre, i_pre, lam, xc):
    r = _sigmoid(r_pre)
    i = _sigmoid(i_pre)
    softplus_neg_lam = jnp.maximum(-lam, 0.0) + jnp.log1p(jnp.exp(-jnp.abs(lam)))
    log_a = (-LRU_C * r) * softplus_neg_lam
    a = jnp.exp(log_a)
    th = jnp.tanh(log_a)
    y = (-2.0 * th) / (1.0 - th)
    root = jnp.where(y == 0.0, 0.0, y * lax.rsqrt(y))
    return a, root * (i * xc)


def _layernorm(x, g, b):
    xc = x - jnp.mean(x, axis=-1, keepdims=True)
    return xc * lax.rsqrt(jnp.mean(xc * xc, axis=-1, keepdims=True) + EPS) * g + b


def _const_spec(shape):
    n = len(shape)
    return pl.BlockSpec(shape, lambda *_: (0,) * n, pipeline_mode=pl.Buffered(1))


def _params(n_grid):
    return pltpu.CompilerParams(dimension_semantics=("arbitrary",) * n_grid,
                                vmem_limit_bytes=VMEM_LIMIT)


def _ffn_tmajor_kernel(x_ref, gpre_ref, wg_ref, wu_ref, wd_ref, gpost_ref, o_ref, *, nsb, tt):
    hs = nsb // 2
    for i in range(2):
        x = x_ref[pl.ds(i * hs, hs)].reshape(hs * tt, D)
        y = _ffn(x, gpre_ref[...], wg_ref, wu_ref, wd_ref, gpost_ref[...])
        o_ref[:, pl.ds(i * hs, hs), :] = jnp.swapaxes(y.reshape(hs, tt, D), 0, 1)


def _ffn_tmajor_call(x3d, gpre, wg, wu, wd, gpost, nsb, tt):
    ns, t, _ = x3d.shape
    return pl.pallas_call(
        functools.partial(_ffn_tmajor_kernel, nsb=nsb, tt=tt),
        grid=(ns // nsb, t // tt),
        in_specs=[pl.BlockSpec((nsb, tt, D), lambda i, j: (i, j, 0)),
                  _const_spec((1, D)), _const_spec((D, D_FF)), _const_spec((D, D_FF)),
                  _const_spec((D_FF, D)), _const_spec((1, D))],
        out_specs=pl.BlockSpec((tt, nsb, D), lambda i, j: (j, i, 0)),
        out_shape=jax.ShapeDtypeStruct((t, ns, D), F32),
        compiler_params=_params(2),
        name="ffn1",
    )(x3d, gpre, wg, wu, wd, gpost)


def _mixer_kernel(*refs, nsb, tt, with_q):
    (x_ref, h4_ref, h0_ref, h31_ref, gpre_ref, win_ref, c4w_ref, c4b_ref, wax_ref, ba_ref, bx_ref,
     lam_ref, c31w_ref, c31b_ref, lng_ref, lnb_ref, wout_ref, gpost_ref) = refs[:18]
    refs = refs[18:]
    if with_q:
        gq_ref, wq_ref, o_ref, q_ref = refs[:4]
        refs = refs[4:]
    else:
        o_ref = refs[0]
        refs = refs[1:]
    nc4_ref, nh_ref, nc31_ref, zb, cb, a_buf, u_buf, c_buf, gelu_buf, h_carry, w4b, w31b = refs

    g = nsb // SUB
    rows = tt * nsb
    hist4, hist31 = C4_W - 1, C31_W - 1
    n_lc = D // LANE
    j = pl.program_id(1)
    last = j == pl.num_programs(1) - 1

    @pl.when((pl.program_id(0) == 0) & (j == 0))
    def _():
        for k in range(C4_W):
            w4b[k] = jnp.broadcast_to(c4w_ref[pl.ds(k, 1), :], (SUB, D))
        for k in range(C31_W):
            for lc in range(n_lc):
                w31b[lc, k] = jnp.broadcast_to(c31w_ref[pl.ds(k, 1), pl.ds(lc * LANE, LANE)],
                                               (SUB, LANE))
        for lc in range(n_lc):
            w31b[lc, C31_W] = jnp.broadcast_to(c31b_ref[:, pl.ds(lc * LANE, LANE)], (SUB, LANE))

    @pl.when(j == 0)
    def _():
        zb[pl.ds(0, hist4)] = h4_ref[...].reshape(hist4, g, SUB, D)
        for lc in range(n_lc):
            cb[lc, pl.ds(0, hist31)] = h31_ref[:, :, pl.ds(lc * LANE, LANE)].reshape(
                hist31, g, SUB, LANE)
        h_carry[...] = h0_ref[...].reshape(g, SUB, D)

    x = x_ref[...].reshape(rows, D)
    h = _rms(x, gpre_ref[...]).astype(BF)

    zb[pl.ds(hist4, tt)] = _dot(h, win_ref[:, pl.ds(0, D)]).reshape(tt, g, SUB, D)
    xc = jnp.broadcast_to(c4b_ref[...].reshape(1, 1, 1, D), (tt, g, SUB, D))
    for k in range(C4_W):
        xc = xc + w4b[k][None, None] * zb[pl.ds(k, tt)]

    zb[pl.ds(0, hist4)] = zb[pl.ds(tt, hist4)]

    xcb = xc.reshape(rows, D).astype(BF)
    xc2 = xc.reshape(rows, D)
    for p in range(N_PAIR):
        sl = slice(p * PAIR_W, (p + 1) * PAIR_W)
        z_v = _dot(h, win_ref[:, pl.ds(2 * D + p * PAIR_W, PAIR_W)])
        z_gate = _dot(h, win_ref[:, pl.ds(3 * D + p * PAIR_W, PAIR_W)])
        g2 = _dot(xcb[:, sl], wax_ref[p])
        a, u = _gate_math(g2[:, :PAIR_W] + ba_ref[:, sl], g2[:, PAIR_W:] + bx_ref[:, sl],
                          lam_ref[:, sl], xc2[:, sl])
        a_buf[:, :, :, sl] = a.reshape(tt, g, SUB, PAIR_W)
        u_buf[:, :, :, sl] = u.reshape(tt, g, SUB, PAIR_W)
        gelu_buf[:, sl] = _gelu_tanh(_dot(h, win_ref[:, pl.ds(D + p * PAIR_W, PAIR_W)]))
        glu = (z_v * _sigmoid(z_gate)).reshape(tt, g, SUB, PAIR_W)
        for i_ in range(PAIR_W // LANE):
            cb[p * (PAIR_W // LANE) + i_, pl.ds(hist31, tt)] = glu[..., i_ * LANE:(i_ + 1) * LANE]

    def scan_body(t, hc):
        hc = a_buf[t] * hc + u_buf[t]
        u_buf[t] = hc
        return hc

    h_carry[...] = lax.fori_loop(0, tt, scan_body, h_carry[...])

    rb = min(tt, CONV_CHUNK)

    def conv_body(lc, carry):
        for k0 in range(0, C31_W, CONV_TAPS):
            nk = min(CONV_TAPS, C31_W - k0)
            wk = [w31b[lc, k0 + i] for i in range(nk)]
            for gi in range(g):
                for t0 in range(0, tt, rb):
                    if k0 == 0:
                        acc = [w31b[lc, C31_W] for _ in range(rb)]
                    else:
                        acc = [c_buf[lc, t0 + r, gi] for r in range(rb)]
                    for s in range(rb + nk - 1):
                        xs = cb[lc, t0 + k0 + s, gi]
                        for r in range(rb):
                            if 0 <= s - r < nk:
                                acc[r] = acc[r] + wk[s - r] * xs
                    for r in range(rb):
                        c_buf[lc, t0 + r, gi] = acc[r]
        return carry

    lax.fori_loop(0, n_lc, conv_body, 0)

    for lc in range(n_lc):
        for s in range(hist31):
            cb[lc, s] = cb[lc, tt + s]

    y_lru = (gelu_buf[...] * u_buf[...].reshape(rows, D)).astype(BF)
    out_lru = _dot(y_lru, wout_ref[pl.ds(0, D), :])
    c = jnp.concatenate([c_buf[lc].reshape(rows, LANE) for lc in range(n_lc)], axis=-1)
    yc = _layernorm(c, lng_ref[...], lnb_ref[...])
    y_conf = (yc * _sigmoid(yc)).astype(BF)
    th_ = tt // 2
    for hf in range(2):
        rs = slice(hf * th_ * nsb, (hf + 1) * th_ * nsb)
        out = out_lru[rs] + _dot(y_conf[rs], wout_ref[pl.ds(D, D), :])
        x2 = x[rs] + _rms(out, gpost_ref[...])
        o_ref[:, pl.ds(hf * th_, th_), :] = jnp.swapaxes(x2.reshape(th_, nsb, D), 0, 1)
        if with_q:
            q = _dot(_rms(x2, gq_ref[...]).astype(BF), wq_ref[...])
            q_ref[:, pl.ds(hf * th_, th_), :] = jnp.swapaxes(q.reshape(th_, nsb, D), 0, 1)

    @pl.when(last)
    def _():
        nc4_ref[...] = zb[pl.ds(0, hist4)].reshape(hist4, nsb, D)
        nh_ref[...] = h_carry[...].reshape(nsb, D)
        for lc in range(n_lc):
            nc31_ref[:, :, pl.ds(lc * LANE, LANE)] = cb[lc, pl.ds(0, hist31)].reshape(
                hist31, nsb, LANE)


def _mixer_call(x_tm, hist4, h0, hist31, mix, q_params, nsb, tt):
    t, ns, _ = x_tm.shape
    with_q = q_params is not None
    vec = _const_spec((1, D))
    seq_major = pl.BlockSpec((nsb, tt, D), lambda i, j: (i, j, 0))
    in_specs = [pl.BlockSpec((tt, nsb, D), lambda i, j: (j, i, 0)),
                pl.BlockSpec((C4_W - 1, nsb, D), lambda i, j: (0, i, 0)),
                pl.BlockSpec((nsb, D), lambda i, j: (i, 0)),
                pl.BlockSpec((C31_W - 1, nsb, D), lambda i, j: (0, i, 0)),
                vec, _const_spec((D, 4 * D)), _const_spec((C4_W, D)), vec,
                _const_spec((N_PAIR, PAIR_W, 2 * PAIR_W)), vec, vec, vec,
                _const_spec((C31_W, D)), vec, vec, vec, _const_spec((2 * D, D)), vec]
    out_specs = [seq_major]
    out_shape = [jax.ShapeDtypeStruct((ns, t, D), F32)]
    args = [x_tm, hist4, h0, hist31, *mix]
    if with_q:
        in_specs += [vec, _const_spec((D, D))]
        out_specs.append(seq_major)
        out_shape.append(jax.ShapeDtypeStruct((ns, t, D), F32))
        args += list(q_params)
    out_specs += [pl.BlockSpec((C4_W - 1, nsb, D), lambda i, j: (0, i, 0)),
                  pl.BlockSpec((nsb, D), lambda i, j: (i, 0)),
                  pl.BlockSpec((C31_W - 1, nsb, D), lambda i, j: (0, i, 0))]
    out_shape += [jax.ShapeDtypeStruct((C4_W - 1, ns, D), F32),
                  jax.ShapeDtypeStruct((ns, D), F32),
                  jax.ShapeDtypeStruct((C31_W - 1, ns, D), F32)]
    g = nsb // SUB
    tile = lambda n: pltpu.VMEM((n, g, SUB, D), F32)
    col = lambda n: pltpu.VMEM((D // LANE, n, g, SUB, LANE), F32)
    return pl.pallas_call(
        functools.partial(_mixer_kernel, nsb=nsb, tt=tt, with_q=with_q),
        grid=(ns // nsb, t // tt),
        in_specs=in_specs, out_specs=out_specs, out_shape=out_shape,
        scratch_shapes=[tile(C4_W - 1 + tt), col(C31_W - 1 + tt), tile(tt), tile(tt), col(tt),
                        pltpu.VMEM((tt * nsb, D), F32),
                        pltpu.VMEM((g, SUB, D), F32),
                        pltpu.VMEM((C4_W, SUB, D), F32),
                        pltpu.VMEM((D // LANE, C31_W + 1, SUB, LANE), F32)],
        compiler_params=_params(2),
        name="mixer",
    )(*args)


def _memkv_kernel(m_ref, g_ref, wk_ref, wv_ref, k5_ref, v5_ref, kb_ref, vb_ref, *, nb):
    m = _rms(m_ref[...], g_ref[...]).astype(BF)
    k = _dot(m, wk_ref[...])
    v = _dot(m, wv_ref[...])
    k5_ref[...] = k.reshape(nb, N_MEM, N_HEADS, HEAD_D)
    v5_ref[...] = v.reshape(nb, N_MEM, N_HEADS, HEAD_D)
    kb_ref[...] = k.astype(BF)
    vb_ref[...] = v.astype(BF)


def _memkv_call(mem2d, g, wk, wv, nb):
    rows = mem2d.shape[0]
    b = rows // N_MEM
    tm = nb * N_MEM
    kv5 = pl.BlockSpec((nb, N_MEM, N_HEADS, HEAD_D), lambda i: (i, 0, 0, 0))
    row = pl.BlockSpec((tm, D), lambda i: (i, 0))
    return pl.pallas_call(
        functools.partial(_memkv_kernel, nb=nb),
        grid=(b // nb,),
        in_specs=[row, _const_spec((1, D)), _const_spec((D, D)), _const_spec((D, D))],
        out_specs=[kv5, kv5, row, row],
        out_shape=[jax.ShapeDtypeStruct((b, N_MEM, N_HEADS, HEAD_D), F32),
                   jax.ShapeDtypeStruct((b, N_MEM, N_HEADS, HEAD_D), F32),
                   jax.ShapeDtypeStruct((rows, D), BF), jax.ShapeDtypeStruct((rows, D), BF)],
        compiler_params=_params(1),
        name="memkv",
    )(mem2d, g, wk, wv)


def _attn_ffn_prompt_kernel(x_ref, k_ref, v_ref, gq_ref, wq_ref, wo_ref, gao_ref,
                            gpre_ref, wg_ref, wu_ref, wd_ref, gpost_ref, o_ref):
    x = x_ref[...]
    q = _dot(_rms(x, gq_ref[...]).astype(BF), wq_ref[...])
    k = k_ref[...]
    v = v_ref[...]
    heads = [slice(hh * HEAD_D, (hh + 1) * HEAD_D) for hh in range(N_HEADS)]
    scores = [lax.dot_general(q[:, sl].astype(BF), k[:, sl], (((1,), (1,)), ((), ())),
                              preferred_element_type=F32) * (HEAD_D ** -0.5) for sl in heads]
    probs = []
    for s in scores:
        e = jnp.exp(s - jnp.max(s, axis=-1, keepdims=True))
        probs.append((e / jnp.sum(e, axis=-1, keepdims=True)).astype(BF))
    o = jnp.concatenate([_dot(p, v[:, sl]) for p, sl in zip(probs, heads)], axis=-1)
    x3 = x + _rms(_dot(o.astype(BF), wo_ref[...]), gao_ref[...])
    hr = x3.shape[0] // 2
    for i in range(2):
        o_ref[pl.ds(i * hr, hr), :] = _ffn(x3[i * hr:(i + 1) * hr], gpre_ref[...], wg_ref, wu_ref,
                                           wd_ref, gpost_ref[...])


def _attn_ffn_prompt_call(x3d, k3d, v3d, gq, wq, wo, gao, gpre, wg, wu, wd, gpost, tm):
    b, s, _ = x3d.shape
    vec = _const_spec((1, D))
    return pl.pallas_call(
        _attn_ffn_prompt_kernel,
        grid=(b, s // tm),
        in_specs=[pl.BlockSpec((None, tm, D), lambda i, j: (i, j, 0)),
                  pl.BlockSpec((None, N_MEM, D), lambda i, j: (i, 0, 0)),
                  pl.BlockSpec((None, N_MEM, D), lambda i, j: (i, 0, 0)),
                  vec, _const_spec((D, D)), _const_spec((D, D)), vec,
                  vec, _const_spec((D, D_FF)), _const_spec((D, D_FF)), _const_spec((D_FF, D)), vec],
        out_specs=pl.BlockSpec((None, tm, D), lambda i, j: (i, j, 0)),
        out_shape=jax.ShapeDtypeStruct((b, s, D), F32),
        compiler_params=_params(2),
        name="attn_ffn_prompt",
    )(x3d, k3d, v3d, gq, wq, wo, gao, gpre, wg, wu, wd, gpost)


def _attn_sample_kernel(q_ref, k_ref, v_ref, o_ref, *, sq):
    rows = N_HEADS * SUB
    cols = N_MEM * N_HEADS
    own_head = (lax.broadcasted_iota(jnp.int32, (rows, cols), 1) % N_HEADS
                == lax.broadcasted_iota(jnp.int32, (rows, cols), 0) // SUB)
    scores = []
    for s in range(sq):
        kn = k_ref[s].reshape(cols, HEAD_D).astype(BF)
        q = q_ref[pl.ds(s * SUB, SUB), :]
        qh = jnp.concatenate([q[:, h * HEAD_D:(h + 1) * HEAD_D] for h in range(N_HEADS)], axis=0)
        scores.append(lax.dot_general(qh.astype(BF), kn, (((1,), (1,)), ((), ())),
                                      preferred_element_type=F32) * (HEAD_D ** -0.5))
    probs = []
    for s in range(sq):
        sc = jnp.where(own_head, scores[s], -1e30)
        e = jnp.exp(sc - jnp.max(sc, axis=-1, keepdims=True))
        probs.append((e / jnp.sum(e, axis=-1, keepdims=True)).astype(BF))
    for s in range(sq):
        vn = v_ref[s].reshape(cols, HEAD_D).astype(BF)
        oh = _dot(probs[s], vn)
        for h in range(N_HEADS):
            o_ref[pl.ds(s * SUB, SUB), pl.ds(h * HEAD_D, HEAD_D)] = oh[h * SUB:(h + 1) * SUB, :]


def _attn_sample_call(q2d, k5, v5, sq):
    n_seq = k5.shape[1]
    rows = sq * SUB
    kv = pl.BlockSpec((None, sq, N_MEM, N_HEADS, HEAD_D), lambda i: (0, i, 0, 0, 0))
    return pl.pallas_call(
        functools.partial(_attn_sample_kernel, sq=sq),
        grid=(n_seq // sq,),
        in_specs=[pl.BlockSpec((rows, D), lambda i: (i, 0)), kv, kv],
        out_specs=pl.BlockSpec((rows, D), lambda i: (i, 0)),
        out_shape=jax.ShapeDtypeStruct((n_seq * SUB, D), F32),
        compiler_params=_params(1),
        name="attn_sample",
    )(q2d, k5, v5)


def _oproj_ffn_kernel(x_ref, a_ref, wo_ref, gao_ref, gpre_ref, wg_ref, wu_ref, wd_ref, gpost_ref,
                      o_ref):
    x3 = x_ref[...] + _rms(_dot(a_ref[...].astype(BF), wo_ref[...]), gao_ref[...])
    o_ref[...] = _ffn(x3, gpre_ref[...], wg_ref, wu_ref, wd_ref, gpost_ref[...])


def _oproj_ffn_call(x2d, a2d, wo, gao, gpre, wg, wu, wd, gpost, tm):
    rows = x2d.shape[0]
    vec = _const_spec((1, D))
    row = pl.BlockSpec((tm, D), lambda i: (i, 0))
    return pl.pallas_call(
        _oproj_ffn_kernel,
        grid=(rows // tm,),
        in_specs=[row, row, _const_spec((D, D)), vec,
                  vec, _const_spec((D, D_FF)), _const_spec((D, D_FF)), _const_spec((D_FF, D)), vec],
        out_specs=row,
        out_shape=jax.ShapeDtypeStruct((rows, D), F32),
        compiler_params=_params(1),
        name="oproj_ffn",
    )(x2d, a2d, wo, gao, gpre, wg, wu, wd, gpost)


def _pair_gate_weights(w_a, w_x):
    hd = w_a.shape[-1]

    def blockdiag(w):
        w = w.reshape(N_PAIR, 2, hd, hd)
        z = jnp.zeros((N_PAIR, hd, hd), w.dtype)
        top = jnp.concatenate([w[:, 0], z], axis=-1)
        bot = jnp.concatenate([z, w[:, 1]], axis=-1)
        return jnp.concatenate([top, bot], axis=-2)

    return jnp.concatenate([blockdiag(w_a), blockdiag(w_x)], axis=-1).astype(BF)


def _seq_first(a):
    return jnp.swapaxes(a, 0, 1)


def _layer(l, xp, xs, mem_prompt, state_lru_conv, state_lru_h, state_conf_conv, cache_mem_k,
           cache_mem_v, w):
    bp, sp, _ = xp.shape
    bs, ss, _ = xs.shape
    vec = lambda name: w[name][l].reshape(1, D)
    bf = lambda name: w[name][l].astype(BF)

    ff1 = (vec('g_ff1_pre'), bf('ff1_w_gate'), bf('ff1_w_up'), bf('ff1_w_down'), vec('g_ff1_post'))
    ff2 = (vec('g_ff2_pre'), bf('ff2_w_gate'), bf('ff2_w_up'), bf('ff2_w_down'), vec('g_ff2_post'))
    mix = (vec('g_mix_pre'), bf('w_in'), w['lru_conv_w'][l], vec('lru_conv_b'),
           _pair_gate_weights(w['lru_w_a'][l], w['lru_w_x'][l]), vec('lru_b_a'), vec('lru_b_x'),
           vec('lru_lambda'), w['conf_conv_w'][l], vec('conf_conv_b'), vec('conf_ln_g'),
           vec('conf_ln_b'), bf('w_out'), vec('g_mix_post'))
    wq, wo = bf('w_q'), bf('w_o')

    mk5, mv5, mkb, mvb = _memkv_call(mem_prompt.reshape(bp * N_MEM, D), vec('g_mem_kv'),
                                     bf('w_mem_k'), bf('w_mem_v'), nb=MEMKV_BATCHES)
    x1 = _ffn_tmajor_call(xp, *ff1, nsb=bp, tt=ROW_TILE // bp)
    x2, c4p, hp, c31p = _mixer_call(
        x1, jnp.zeros((C4_W - 1, bp, D), F32), jnp.zeros((bp, D), F32),
        jnp.zeros((C31_W - 1, bp, D), F32), mix, None, nsb=bp, tt=PROMPT_TIME_TILE)
    xp_out = _attn_ffn_prompt_call(x2, mkb.reshape(bp, N_MEM, D), mvb.reshape(bp, N_MEM, D),
                                   vec('g_mem_pre'), wq, wo, vec('g_mem_post'), *ff2, tm=ROW_TILE)

    s1 = _ffn_tmajor_call(xs, *ff1, nsb=ROW_TILE // ss, tt=ss)
    s2, q, c4s, hs, c31s = _mixer_call(
        s1, _seq_first(state_lru_conv[l]), state_lru_h[l], _seq_first(state_conf_conv[l]), mix,
        (vec('g_mem_pre'), wq), nsb=SAMPLE_MIXER_SEQS, tt=ss)
    att = _attn_sample_call(q.reshape(bs * ss, D), cache_mem_k[l:l + 1], cache_mem_v[l:l + 1],
                            sq=SAMPLE_ATTN_SEQS)
    xs_out = _oproj_ffn_call(s2.reshape(bs * ss, D), att, wo, vec('g_mem_post'), *ff2,
                             tm=ROW_TILE).reshape(bs, ss, D)

    return (xp_out, xs_out, _seq_first(c4p), hp, _seq_first(c31p), mk5, mv5,
            _seq_first(c4s), hs, _seq_first(c31s))


def kernel(x_prompt, x_sample, mem_prompt, state_lru_conv, state_lru_h, state_conf_conv, cache_mem_k, cache_mem_v, g_ff1_pre, ff1_w_gate, ff1_w_up, ff1_w_down, g_ff1_post, g_mix_pre, w_in, lru_conv_w, lru_conv_b, lru_w_a, lru_b_a, lru_w_x, lru_b_x, lru_lambda, conf_conv_w, conf_conv_b, conf_ln_g, conf_ln_b, w_out, g_mix_post, g_mem_pre, g_mem_kv, w_mem_k, w_mem_v, w_q, w_o, g_mem_post, g_ff2_pre, ff2_w_gate, ff2_w_up, ff2_w_down, g_ff2_post):
    w = dict(g_ff1_pre=g_ff1_pre, ff1_w_gate=ff1_w_gate, ff1_w_up=ff1_w_up, ff1_w_down=ff1_w_down,
             g_ff1_post=g_ff1_post, g_mix_pre=g_mix_pre, w_in=w_in, lru_conv_w=lru_conv_w,
             lru_conv_b=lru_conv_b, lru_w_a=lru_w_a, lru_b_a=lru_b_a, lru_w_x=lru_w_x,
             lru_b_x=lru_b_x, lru_lambda=lru_lambda, conf_conv_w=conf_conv_w,
             conf_conv_b=conf_conv_b, conf_ln_g=conf_ln_g, conf_ln_b=conf_ln_b, w_out=w_out,
             g_mix_post=g_mix_post, g_mem_pre=g_mem_pre, g_mem_kv=g_mem_kv, w_mem_k=w_mem_k,
             w_mem_v=w_mem_v, w_q=w_q, w_o=w_o, g_mem_post=g_mem_post, g_ff2_pre=g_ff2_pre,
             ff2_w_gate=ff2_w_gate, ff2_w_up=ff2_w_up, ff2_w_down=ff2_w_down, g_ff2_post=g_ff2_post)
    depth = w_in.shape[0]
    xp, xs = x_prompt, x_sample
    per_layer = []
    for l in range(depth):
        outs = _layer(l, xp, xs, mem_prompt, state_lru_conv, state_lru_h, state_conf_conv,
                      cache_mem_k, cache_mem_v, w)
        xp, xs = outs[0], outs[1]
        per_layer.append(outs[2:])
    stacked = [jnp.stack([pl_[i] for pl_ in per_layer]) for i in range(8)]
    return (xp, xs, *stacked)
```

```python
import functools

import jax
import jax.numpy as jnp
from jax import lax
from jax.experimental import pallas as pl
from jax.experimental.pallas import tpu as pltpu

D = 1024
D_FF = 2816
N_PAIR = 4
PAIR_W = 256
LRU_C = 8.0
C4_W = 4
C31_W = 31
N_MEM = 256
N_HEADS = 4
HEAD_D = 256
FFN_RES = 0.5
EPS = 1e-6
SUB = 8
LANE = 128
CONV_CHUNK = 8
CONV_TAPS = 16
ROW_TILE = 512
PROMPT_TIME_TILE = 64
SAMPLE_MIXER_SEQS = 32
SAMPLE_ATTN_SEQS = 8
MEMKV_BATCHES = 2
VMEM_LIMIT = 56 * 1024 * 1024

BF = jnp.bfloat16
F32 = jnp.float32


def _dot(a, b):
    return jnp.dot(a, b, preferred_element_type=F32)


def _rms(x, g):
    ms = jnp.mean(x * x, axis=-1, keepdims=True)
    return x * lax.rsqrt(ms + EPS) * g


def _sigmoid(x):
    return 1.0 / (1.0 + jnp.exp(-x))


def _gelu_tanh(x):
    return 0.5 * x * (1.0 + jnp.tanh(0.7978845608028654 * (x + 0.044715 * (x * x * x))))


def _ffn(x, gpre, wg_ref, wu_ref, wd_ref, gpost):
    h = _rms(x, gpre).astype(BF)
    gate = _dot(h, wg_ref[...])
    up = _dot(h, wu_ref[...])
    act = (gate * _sigmoid(gate) * up).astype(BF)
    y = _dot(act, wd_ref[...])
    return x + FFN_RES * _rms(y, gpost)


def _gate_math(r_pre, i_pre, lam, xc):
    r = _sigmoid(r_pre)
    i = _sigmoid(i_pre)
    softplus_neg_lam = jnp.maximum(-lam, 0.0) + jnp.log1p(jnp.exp(-jnp.abs(lam)))
    log_a = (-LRU_C * r) * softplus_neg_lam
    a = jnp.exp(log_a)
    th = jnp.tanh(log_a)
    y = (-2.0 * th) / (1.0 - th)
    root = jnp.where(y == 0.0, 0.0, y * lax.rsqrt(y))
    return a, root * (i * xc)


def _layernorm(x, g, b):
    xc = x - jnp.mean(x, axis=-1, keepdims=True)
    return xc * lax.rsqrt(jnp.mean(xc * xc, axis=-1, keepdims=True) + EPS) * g + b


def _const_spec(shape):
    n = len(shape)
    return pl.BlockSpec(shape, lambda *_: (0,) * n, pipeline_mode=pl.Buffered(1))


def _params(n_grid):
    return pltpu.CompilerParams(dimension_semantics=("arbitrary",) * n_grid,
                                vmem_limit_bytes=VMEM_LIMIT)


def _ffn_tmajor_kernel(x_ref, gpre_ref, wg_ref, wu_ref, wd_ref, gpost_ref, o_ref, *, nsb, tt):
    hs = nsb // 2
    ys = [_ffn(x_ref[pl.ds(i * hs, hs)].reshape(hs * tt, D), gpre_ref[...], wg_ref, wu_ref, wd_ref,
               gpost_ref[...]) for i in range(2)]
    o_ref[...] = jnp.swapaxes(jnp.concatenate(ys, axis=0).reshape(nsb, tt, D), 0, 1)


def _ffn_tmajor_call(x3d, gpre, wg, wu, wd, gpost, nsb, tt):
    ns, t, _ = x3d.shape
    return pl.pallas_call(
        functools.partial(_ffn_tmajor_kernel, nsb=nsb, tt=tt),
        grid=(ns // nsb, t // tt),
        in_specs=[pl.BlockSpec((nsb, tt, D), lambda i, j: (i, j, 0)),
                  _const_spec((1, D)), _const_spec((D, D_FF)), _const_spec((D, D_FF)),
                  _const_spec((D_FF, D)), _const_spec((1, D))],
        out_specs=pl.BlockSpec((tt, nsb, D), lambda i, j: (j, i, 0)),
        out_shape=jax.ShapeDtypeStruct((t, ns, D), F32),
        compiler_params=_params(2),
        name="ffn1",
    )(x3d, gpre, wg, wu, wd, gpost)


def _mixer_kernel(*refs, nsb, tt, with_q):
    (x_ref, h4_ref, h0_ref, h31_ref, gpre_ref, win_ref, c4w_ref, c4b_ref, wax_ref, ba_ref, bx_ref,
     lam_ref, c31w_ref, c31b_ref, lng_ref, lnb_ref, wout_ref, gpost_ref) = refs[:18]
    refs = refs[18:]
    if with_q:
        gq_ref, wq_ref, o_ref, q_ref = refs[:4]
        refs = refs[4:]
    else:
        o_ref = refs[0]
        refs = refs[1:]
    nc4_ref, nh_ref, nc31_ref, zb, cb, a_buf, u_buf, c_buf, gelu_buf, h_carry, w4b, w31b = refs

    g = nsb // SUB
    rows = tt * nsb
    hist4, hist31 = C4_W - 1, C31_W - 1
    n_lc = D // LANE
    j = pl.program_id(1)
    last = j == pl.num_programs(1) - 1

    @pl.when((pl.program_id(0) == 0) & (j == 0))
    def _():
        for k in range(C4_W):
            w4b[k] = jnp.broadcast_to(c4w_ref[pl.ds(k, 1), :], (SUB, D))
        for k in range(C31_W):
            for lc in range(n_lc):
                w31b[lc, k] = jnp.broadcast_to(c31w_ref[pl.ds(k, 1), pl.ds(lc * LANE, LANE)],
                                               (SUB, LANE))
        for lc in range(n_lc):
            w31b[lc, C31_W] = jnp.broadcast_to(c31b_ref[:, pl.ds(lc * LANE, LANE)], (SUB, LANE))

    @pl.when(j == 0)
    def _():
        zb[pl.ds(0, hist4)] = h4_ref[...].reshape(hist4, g, SUB, D)
        for lc in range(n_lc):
            cb[lc, pl.ds(0, hist31)] = h31_ref[:, :, pl.ds(lc * LANE, LANE)].reshape(
                hist31, g, SUB, LANE)
        h_carry[...] = h0_ref[...].reshape(g, SUB, D)

    x = x_ref[...].reshape(rows, D)
    h = _rms(x, gpre_ref[...]).astype(BF)

    zb[pl.ds(hist4, tt)] = _dot(h, win_ref[:, pl.ds(0, D)]).reshape(tt, g, SUB, D)
    xc = jnp.broadcast_to(c4b_ref[...].reshape(1, 1, 1, D), (tt, g, SUB, D))
    for k in range(C4_W):
        xc = xc + w4b[k][None, None] * zb[pl.ds(k, tt)]

    zb[pl.ds(0, hist4)] = zb[pl.ds(tt, hist4)]

    xcb = xc.reshape(rows, D).astype(BF)
    xc2 = xc.reshape(rows, D)
    for p in range(N_PAIR):
        sl = slice(p * PAIR_W, (p + 1) * PAIR_W)
        z_v = _dot(h, win_ref[:, pl.ds(2 * D + p * PAIR_W, PAIR_W)])
        z_gate = _dot(h, win_ref[:, pl.ds(3 * D + p * PAIR_W, PAIR_W)])
        g2 = _dot(xcb[:, sl], wax_ref[p])
        a, u = _gate_math(g2[:, :PAIR_W] + ba_ref[:, sl], g2[:, PAIR_W:] + bx_ref[:, sl],
                          lam_ref[:, sl], xc2[:, sl])
        a_buf[:, :, :, sl] = a.reshape(tt, g, SUB, PAIR_W)
        u_buf[:, :, :, sl] = u.reshape(tt, g, SUB, PAIR_W)
        gelu_buf[:, sl] = _gelu_tanh(_dot(h, win_ref[:, pl.ds(D + p * PAIR_W, PAIR_W)]))
        glu = (z_v * _sigmoid(z_gate)).reshape(tt, g, SUB, PAIR_W)
        for i_ in range(PAIR_W // LANE):
            cb[p * (PAIR_W // LANE) + i_, pl.ds(hist31, tt)] = glu[..., i_ * LANE:(i_ + 1) * LANE]

    def scan_body(t, hc):
        hc = a_buf[t] * hc + u_buf[t]
        u_buf[t] = hc
        return hc

    h_carry[...] = lax.fori_loop(0, tt, scan_body, h_carry[...])

    rb = min(tt, CONV_CHUNK)

    def conv_body(lc, carry):
        for k0 in range(0, C31_W, CONV_TAPS):
            nk = min(CONV_TAPS, C31_W - k0)
            wk = [w31b[lc, k0 + i] for i in range(nk)]
            for gi in range(g):
                for t0 in range(0, tt, rb):
                    if k0 == 0:
                        acc = [w31b[lc, C31_W] for _ in range(rb)]
                    else:
                        acc = [c_buf[lc, t0 + r, gi] for r in range(rb)]
                    for s in range(rb + nk - 1):
                        xs = cb[lc, t0 + k0 + s, gi]
                        for r in range(rb):
                            if 0 <= s - r < nk:
                                acc[r] = acc[r] + wk[s - r] * xs
                    for r in range(rb):
                        c_buf[lc, t0 + r, gi] = acc[r]
        return carry

    lax.fori_loop(0, n_lc, conv_body, 0)

    for lc in range(n_lc):
        for s in range(hist31):
            cb[lc, s] = cb[lc, tt + s]

    y_lru = (gelu_buf[...] * u_buf[...].reshape(rows, D)).astype(BF)
    out_lru = _dot(y_lru, wout_ref[pl.ds(0, D), :])
    c = jnp.concatenate([c_buf[lc].reshape(rows, LANE) for lc in range(n_lc)], axis=-1)
    yc = _layernorm(c, lng_ref[...], lnb_ref[...])
    y_conf = (yc * _sigmoid(yc)).astype(BF)
    th_ = tt // 2
    for hf in range(2):
        rs = slice(hf * th_ * nsb, (hf + 1) * th_ * nsb)
        out = out_lru[rs] + _dot(y_conf[rs], wout_ref[pl.ds(D, D), :])
        x2 = x[rs] + _rms(out, gpost_ref[...])
        o_ref[:, pl.ds(hf * th_, th_), :] = jnp.swapaxes(x2.reshape(th_, nsb, D), 0, 1)
        if with_q:
            q = _dot(_rms(x2, gq_ref[...]).astype(BF), wq_ref[...])
            q_ref[:, pl.ds(hf * th_, th_), :] = jnp.swapaxes(q.reshape(th_, nsb, D), 0, 1)

    @pl.when(last)
    def _():
        nc4_ref[...] = zb[pl.ds(0, hist4)].reshape(hist4, nsb, D)
        nh_ref[...] = h_carry[...].reshape(nsb, D)
        for lc in range(n_lc):
            nc31_ref[:, :, pl.ds(lc * LANE, LANE)] = cb[lc, pl.ds(0, hist31)].reshape(
                hist31, nsb, LANE)


def _mixer_call(x_tm, hist4, h0, hist31, mix, q_params, nsb, tt):
    t, ns, _ = x_tm.shape
    with_q = q_params is not None
    vec = _const_spec((1, D))
    seq_major = pl.BlockSpec((nsb, tt, D), lambda i, j: (i, j, 0))
    in_specs = [pl.BlockSpec((tt, nsb, D), lambda i, j: (j, i, 0)),
                pl.BlockSpec((C4_W - 1, nsb, D), lambda i, j: (0, i, 0)),
                pl.BlockSpec((nsb, D), lambda i, j: (i, 0)),
                pl.BlockSpec((C31_W - 1, nsb, D), lambda i, j: (0, i, 0)),
                vec, _const_spec((D, 4 * D)), _const_spec((C4_W, D)), vec,
                _const_spec((N_PAIR, PAIR_W, 2 * PAIR_W)), vec, vec, vec,
                _const_spec((C31_W, D)), vec, vec, vec, _const_spec((2 * D, D)), vec]
    out_specs = [seq_major]
    out_shape = [jax.ShapeDtypeStruct((ns, t, D), F32)]
    args = [x_tm, hist4, h0, hist31, *mix]
    if with_q:
        in_specs += [vec, _const_spec((D, D))]
        out_specs.append(seq_major)
        out_shape.append(jax.ShapeDtypeStruct((ns, t, D), F32))
        args += list(q_params)
    out_specs += [pl.BlockSpec((C4_W - 1, nsb, D), lambda i, j: (0, i, 0)),
                  pl.BlockSpec((nsb, D), lambda i, j: (i, 0)),
                  pl.BlockSpec((C31_W - 1, nsb, D), lambda i, j: (0, i, 0))]
    out_shape += [jax.ShapeDtypeStruct((C4_W - 1, ns, D), F32),
                  jax.ShapeDtypeStruct((ns, D), F32),
                  jax.ShapeDtypeStruct((C31_W - 1, ns, D), F32)]
    g = nsb // SUB
    tile = lambda n: pltpu.VMEM((n, g, SUB, D), F32)
    col = lambda n: pltpu.VMEM((D // LANE, n, g, SUB, LANE), F32)
    return pl.pallas_call(
        functools.partial(_mixer_kernel, nsb=nsb, tt=tt, with_q=with_q),
        grid=(ns // nsb, t // tt),
        in_specs=in_specs, out_specs=out_specs, out_shape=out_shape,
        scratch_shapes=[tile(C4_W - 1 + tt), col(C31_W - 1 + tt), tile(tt), tile(tt), col(tt),
                        pltpu.VMEM((tt * nsb, D), F32),
                        pltpu.VMEM((g, SUB, D), F32),
                        pltpu.VMEM((C4_W, SUB, D), F32),
                        pltpu.VMEM((D // LANE, C31_W + 1, SUB, LANE), F32)],
        compiler_params=_params(2),
        name="mixer",
    )(*args)


def _memkv_kernel(m_ref, g_ref, wk_ref, wv_ref, k5_ref, v5_ref, kb_ref, vb_ref, *, nb):
    m = _rms(m_ref[...], g_ref[...]).astype(BF)
    k = _dot(m, wk_ref[...])
    v = _dot(m, wv_ref[...])
    k5_ref[...] = k.reshape(nb, N_MEM, N_HEADS, HEAD_D)
    v5_ref[...] = v.reshape(nb, N_MEM, N_HEADS, HEAD_D)
    kb_ref[...] = k.astype(BF)
    vb_ref[...] = v.astype(BF)


def _memkv_call(mem2d, g, wk, wv, nb):
    rows = mem2d.shape[0]
    b = rows // N_MEM
    tm = nb * N_MEM
    kv5 = pl.BlockSpec((nb, N_MEM, N_HEADS, HEAD_D), lambda i: (i, 0, 0, 0))
    row = pl.BlockSpec((tm, D), lambda i: (i, 0))
    return pl.pallas_call(
        functools.partial(_memkv_kernel, nb=nb),
        grid=(b // nb,),
        in_specs=[row, _const_spec((1, D)), _const_spec((D, D)), _const_spec((D, D))],
        out_specs=[kv5, kv5, row, row],
        out_shape=[jax.ShapeDtypeStruct((b, N_MEM, N_HEADS, HEAD_D), F32),
                   jax.ShapeDtypeStruct((b, N_MEM, N_HEADS, HEAD_D), F32),
                   jax.ShapeDtypeStruct((rows, D), BF), jax.ShapeDtypeStruct((rows, D), BF)],
        compiler_params=_params(1),
        name="memkv",
    )(mem2d, g, wk, wv)


def _attn_ffn_prompt_kernel(x_ref, k_ref, v_ref, gq_ref, wq_ref, wo_ref, gao_ref,
                            gpre_ref, wg_ref, wu_ref, wd_ref, gpost_ref, o_ref):
    x = x_ref[...]
    q = _dot(_rms(x, gq_ref[...]).astype(BF), wq_ref[...])
    k = k_ref[...]
    v = v_ref[...]
    heads = [slice(hh * HEAD_D, (hh + 1) * HEAD_D) for hh in range(N_HEADS)]
    scores = [lax.dot_general(q[:, sl].astype(BF), k[:, sl], (((1,), (1,)), ((), ())),
                              preferred_element_type=F32) * (HEAD_D ** -0.5) for sl in heads]
    probs = []
    for s in scores:
        e = jnp.exp(s - jnp.max(s, axis=-1, keepdims=True))
        probs.append((e / jnp.sum(e, axis=-1, keepdims=True)).astype(BF))
    o = jnp.concatenate([_dot(p, v[:, sl]) for p, sl in zip(probs, heads)], axis=-1)
    x3 = x + _rms(_dot(o.astype(BF), wo_ref[...]), gao_ref[...])
    hr = x3.shape[0] // 2
    for i in range(2):
        o_ref[pl.ds(i * hr, hr), :] = _ffn(x3[i * hr:(i + 1) * hr], gpre_ref[...], wg_ref, wu_ref,
                                           wd_ref, gpost_ref[...])


def _attn_ffn_prompt_call(x3d, k3d, v3d, gq, wq, wo, gao, gpre, wg, wu, wd, gpost, tm):
    b, s, _ = x3d.shape
    vec = _const_spec((1, D))
    return pl.pallas_call(
        _attn_ffn_prompt_kernel,
        grid=(b, s // tm),
        in_specs=[pl.BlockSpec((None, tm, D), lambda i, j: (i, j, 0)),
                  pl.BlockSpec((None, N_MEM, D), lambda i, j: (i, 0, 0)),
                  pl.BlockSpec((None, N_MEM, D), lambda i, j: (i, 0, 0)),
                  vec, _const_spec((D, D)), _const_spec((D, D)), vec,
                  vec, _const_spec((D, D_FF)), _const_spec((D, D_FF)), _const_spec((D_FF, D)), vec],
        out_specs=pl.BlockSpec((None, tm, D), lambda i, j: (i, j, 0)),
        out_shape=jax.ShapeDtypeStruct((b, s, D), F32),
        compiler_params=_params(2),
        name="attn_ffn_prompt",
    )(x3d, k3d, v3d, gq, wq, wo, gao, gpre, wg, wu, wd, gpost)


def _attn_sample_kernel(q_ref, k_ref, v_ref, o_ref, *, sq):
    rows = N_HEADS * SUB
    cols = N_MEM * N_HEADS
    own_head = (lax.broadcasted_iota(jnp.int32, (rows, cols), 1) % N_HEADS
                == lax.broadcasted_iota(jnp.int32, (rows, cols), 0) // SUB)
    scores = []
    for s in range(sq):
        kn = k_ref[s].reshape(cols, HEAD_D).astype(BF)
        q = q_ref[pl.ds(s * SUB, SUB), :]
        qh = jnp.concatenate([q[:, h * HEAD_D:(h + 1) * HEAD_D] for h in range(N_HEADS)], axis=0)
        scores.append(lax.dot_general(qh.astype(BF), kn, (((1,), (1,)), ((), ())),
                                      preferred_element_type=F32) * (HEAD_D ** -0.5))
    probs = []
    for s in range(sq):
        sc = jnp.where(own_head, scores[s], -1e30)
        e = jnp.exp(sc - jnp.max(sc, axis=-1, keepdims=True))
        probs.append((e / jnp.sum(e, axis=-1, keepdims=True)).astype(BF))
    for s in range(sq):
        vn = v_ref[s].reshape(cols, HEAD_D).astype(BF)
        oh = _dot(probs[s], vn)
        for h in range(N_HEADS):
            o_ref[pl.ds(s * SUB, SUB), pl.ds(h * HEAD_D, HEAD_D)] = oh[h * SUB:(h + 1) * SUB, :]


def _attn_sample_call(q2d, k5, v5, sq):
    n_seq = k5.shape[1]
    rows = sq * SUB
    kv = pl.BlockSpec((None, sq, N_MEM, N_HEADS, HEAD_D), lambda i: (0, i, 0, 0, 0))
    return pl.pallas_call(
        functools.partial(_attn_sample_kernel, sq=sq),
        grid=(n_seq // sq,),
        in_specs=[pl.BlockSpec((rows, D), lambda i: (i, 0)), kv, kv],
        out_specs=pl.BlockSpec((rows, D), lambda i: (i, 0)),
        out_shape=jax.ShapeDtypeStruct((n_seq * SUB, D), F32),
        compiler_params=_params(1),
        name="attn_sample",
    )(q2d, k5, v5)


def _oproj_ffn_kernel(x_ref, a_ref, wo_ref, gao_ref, gpre_ref, wg_ref, wu_ref, wd_ref, gpost_ref,
                      o_ref):
    x3 = x_ref[...] + _rms(_dot(a_ref[...].astype(BF), wo_ref[...]), gao_ref[...])
    o_ref[...] = _ffn(x3, gpre_ref[...], wg_ref, wu_ref, wd_ref, gpost_ref[...])


def _oproj_ffn_call(x2d, a2d, wo, gao, gpre, wg, wu, wd, gpost, tm):
    rows = x2d.shape[0]
    vec = _const_spec((1, D))
    row = pl.BlockSpec((tm, D), lambda i: (i, 0))
    return pl.pallas_call(
        _oproj_ffn_kernel,
        grid=(rows // tm,),
        in_specs=[row, row, _const_spec((D, D)), vec,
                  vec, _const_spec((D, D_FF)), _const_spec((D, D_FF)), _const_spec((D_FF, D)), vec],
        out_specs=row,
        out_shape=jax.ShapeDtypeStruct((rows, D), F32),
        compiler_params=_params(1),
        name="oproj_ffn",
    )(x2d, a2d, wo, gao, gpre, wg, wu, wd, gpost)


def _pair_gate_weights(w_a, w_x):
    hd = w_a.shape[-1]

    def blockdiag(w):
        w = w.reshape(N_PAIR, 2, hd, hd)
        z = jnp.zeros((N_PAIR, hd, hd), w.dtype)
        top = jnp.concatenate([w[:, 0], z], axis=-1)
        bot = jnp.concatenate([z, w[:, 1]], axis=-1)
        return jnp.concatenate([top, bot], axis=-2)

    return jnp.concatenate([blockdiag(w_a), blockdiag(w_x)], axis=-1).astype(BF)


def _seq_first(a):
    return jnp.swapaxes(a, 0, 1)


def _layer(l, xp, xs, mem_prompt, state_lru_conv, state_lru_h, state_conf_conv, cache_mem_k,
           cache_mem_v, w):
    bp, sp, _ = xp.shape
    bs, ss, _ = xs.shape
    vec = lambda name: w[name][l].reshape(1, D)
    bf = lambda name: w[name][l].astype(BF)

    ff1 = (vec('g_ff1_pre'), bf('ff1_w_gate'), bf('ff1_w_up'), bf('ff1_w_down'), vec('g_ff1_post'))
    ff2 = (vec('g_ff2_pre'), bf('ff2_w_gate'), bf('ff2_w_up'), bf('ff2_w_down'), vec('g_ff2_post'))
    mix = (vec('g_mix_pre'), bf('w_in'), w['lru_conv_w'][l], vec('lru_conv_b'),
           _pair_gate_weights(w['lru_w_a'][l], w['lru_w_x'][l]), vec('lru_b_a'), vec('lru_b_x'),
           vec('lru_lambda'), w['conf_conv_w'][l], vec('conf_conv_b'), vec('conf_ln_g'),
           vec('conf_ln_b'), bf('w_out'), vec('g_mix_post'))
    wq, wo = bf('w_q'), bf('w_o')

    mk5, mv5, mkb, mvb = _memkv_call(mem_prompt.reshape(bp * N_MEM, D), vec('g_mem_kv'),
                                     bf('w_mem_k'), bf('w_mem_v'), nb=MEMKV_BATCHES)
    x1 = _ffn_tmajor_call(xp, *ff1, nsb=bp, tt=ROW_TILE // bp)
    x2, c4p, hp, c31p = _mixer_call(
        x1, jnp.zeros((C4_W - 1, bp, D), F32), jnp.zeros((bp, D), F32),
        jnp.zeros((C31_W - 1, bp, D), F32), mix, None, nsb=bp, tt=PROMPT_TIME_TILE)
    xp_out = _attn_ffn_prompt_call(x2, mkb.reshape(bp, N_MEM, D), mvb.reshape(bp, N_MEM, D),
                                   vec('g_mem_pre'), wq, wo, vec('g_mem_post'), *ff2, tm=ROW_TILE)

    s1 = _ffn_tmajor_call(xs, *ff1, nsb=ROW_TILE // ss, tt=ss)
    s2, q, c4s, hs, c31s = _mixer_call(
        s1, _seq_first(state_lru_conv[l]), state_lru_h[l], _seq_first(state_conf_conv[l]), mix,
        (vec('g_mem_pre'), wq), nsb=SAMPLE_MIXER_SEQS, tt=ss)
    att = _attn_sample_call(q.reshape(bs * ss, D), cache_mem_k[l:l + 1], cache_mem_v[l:l + 1],
                            sq=SAMPLE_ATTN_SEQS)
    xs_out = _oproj_ffn_call(s2.reshape(bs * ss, D), att, wo, vec('g_mem_post'), *ff2,
                             tm=ROW_TILE).reshape(bs, ss, D)

    return (xp_out, xs_out, _seq_first(c4p), hp, _seq_first(c31p), mk5, mv5,
            _seq_first(c4s), hs, _seq_first(c31s))


def kernel(x_prompt, x_sample, mem_prompt, state_lru_conv, state_lru_h, state_conf_conv, cache_mem_k, cache_mem_v, g_ff1_pre, ff1_w_gate, ff1_w_up, ff1_w_down, g_ff1_post, g_mix_pre, w_in, lru_conv_w, lru_conv_b, lru_w_a, lru_b_a, lru_w_x, lru_b_x, lru_lambda, conf_conv_w, conf_conv_b, conf_ln_g, conf_ln_b, w_out, g_mix_post, g_mem_pre, g_mem_kv, w_mem_k, w_mem_v, w_q, w_o, g_mem_post, g_ff2_pre, ff2_w_gate, ff2_w_up, ff2_w_down, g_ff2_post):
    w = dict(g_ff1_pre=g_ff1_pre, ff1_w_gate=ff1_w_gate, ff1_w_up=ff1_w_up, ff1_w_down=ff1_w_down,
             g_ff1_post=g_ff1_post, g_mix_pre=g_mix_pre, w_in=w_in, lru_conv_w=lru_conv_w,
             lru_conv_b=lru_conv_b, lru_w_a=lru_w_a, lru_b_a=lru_b_a, lru_w_x=lru_w_x,
             lru_b_x=lru_b_x, lru_lambda=lru_lambda, conf_conv_w=conf_conv_w,
             conf_conv_b=conf_conv_b, conf_ln_g=conf_ln_g, conf_ln_b=conf_ln_b, w_out=w_out,
             g_mix_post=g_mix_post, g_mem_pre=g_mem_pre, g_mem_kv=g_mem_kv, w_mem_k=w_mem_k,
             w_mem_v=w_mem_v, w_q=w_q, w_o=w_o, g_mem_post=g_mem_post, g_ff2_pre=g_ff2_pre,
             ff2_w_gate=ff2_w_gate, ff2_w_up=ff2_w_up, ff2_w_down=ff2_w_down, g_ff2_post=g_ff2_post)
    depth = w_in.shape[0]
    xp, xs = x_prompt, x_sample
    per_layer = []
    for l in range(depth):
        outs = _layer(l, xp, xs, mem_prompt, state_lru_conv, state_lru_h, state_conf_conv,
                      cache_mem_k, cache_mem_v, w)
        xp, xs = outs[0], outs[1]
        per_layer.append(outs[2:])
    stacked = [jnp.stack([pl_[i] for pl_ in per_layer]) for i in range(8)]
    return (xp, xs, *stacked)
```
